```python
import jax, jax.numpy as jnp
from jax import lax
import numpy as np

D_MODEL = 1024
BATCH = 4
SEQ = 4096
DEPTH = 4
DEC_BATCH = 8
DEC_SEQ = 32
PAST_LEN = 2048

CHUNK = 64
N_MIXERS = 4
EXPAND = 2
D_INNER = EXPAND * D_MODEL
MLP_BLOCK = 128
MLP_GROUPS = 8
MLP_GROUP_W = D_INNER // MLP_GROUPS
SCONV_W = 3
CCONV_W = 31
POOL_WINDOWS = (2, 4, 8, 16)
POOL_GROUPS = len(POOL_WINDOWS)
POOL_GROUP_W = D_INNER // POOL_GROUPS
POOL_HIST = max(POOL_WINDOWS) - 1
EPS = 1e-6

kernel_name = "hybrid_streaming_encoder_step"


def rms_norm(x, g):
    xf = x.astype(jnp.float32)
    y = xf * lax.rsqrt(jnp.mean(xf * xf, axis=-1, keepdims=True) + EPS)
    return (y * g.astype(jnp.float32)).astype(x.dtype)


def layer_norm(x, g, b):
    xf = x.astype(jnp.float32)
    mu = jnp.mean(xf, axis=-1, keepdims=True)
    var = jnp.mean(jnp.square(xf - mu), axis=-1, keepdims=True)
    y = (xf - mu) * lax.rsqrt(var + EPS)
    return (y * g.astype(jnp.float32) + b.astype(jnp.float32)).astype(x.dtype)


def ada_modulation(c, w, b):
    mod = jnp.einsum('bd,de->be', jax.nn.silu(c), w) + b
    shift, scale, gate = jnp.split(mod, 3, axis=-1)
    return shift[:, None, :], scale[:, None, :], gate[:, None, :]


def with_history(x, hist, n_hist):
    if hist is None:
        ext = jnp.pad(x, ((0, 0), (n_hist, 0), (0, 0)))
    else:
        ext = jnp.concatenate([hist.astype(x.dtype), x], axis=1)
    return ext, ext[:, ext.shape[1] - n_hist:, :]


def causal_depthwise_conv(x_ext, w):
    k_w, ch = w.shape
    return lax.conv_general_dilated(
        x_ext, w[:, None, :].astype(x_ext.dtype), window_strides=(1,), padding='VALID',
        dimension_numbers=('NWC', 'WIO', 'NWC'), feature_group_count=ch)


def chunk_mlp_mix(v, w_s, b_s):
    bsz, s_len, e = v.shape
    blk = MLP_BLOCK if s_len >= MLP_BLOCK else s_len
    nb = s_len // blk
    cidx = jnp.arange(blk) // CHUNK
    mask = cidx[:, None] >= cidx[None, :]
    w = jnp.where(mask[None], w_s[:, :blk, :blk], jnp.zeros((), w_s.dtype))
    vb = v.reshape(bsz, nb, blk, MLP_GROUPS, MLP_GROUP_W)
    out = jnp.einsum('gij,bnjgc->bnigc', w, vb) + jnp.transpose(b_s[:, :blk])[None, None, :, :, None]
    return out.reshape(bsz, s_len, e)


def multi_scale_pool(ext, pos0, n_out):
    bsz, t_len, e = ext.shape
    xg = ext.astype(jnp.float32).reshape(bsz, t_len, POOL_GROUPS, POOL_GROUP_W)
    cs = jnp.pad(jnp.cumsum(xg, axis=1), ((0, 0), (1, 0), (0, 0), (0, 0)))
    hi = cs[:, POOL_HIST + 1:POOL_HIST + 1 + n_out]
    pos = pos0 + jnp.arange(n_out)
    means = []
    for g, win in enumerate(POOL_WINDOWS):
        lo = cs[:, POOL_HIST + 1 - win:POOL_HIST + 1 - win + n_out, g]
        cnt = jnp.minimum(win, pos + 1).astype(jnp.float32)
        means.append((hi[:, :, g] - lo) / cnt[None, :, None])
    return jnp.stack(means, axis=2).reshape(bsz, n_out, e).astype(ext.dtype)


def run_trunk(x, c, hist_b, hist_c, hist_d, pos0,
              w_ada, b_ada, g_norm,
              w_in_a, ln_a_g, ln_a_b, w_s_a, b_s_a, w_out_a,
              w_in_b, w_conv_b, w_out_b,
              w_in_c, w_conv_c, b_conv_c, ln_c_g, ln_c_b, w_out_c,
              w_in_d, w_pool_d, scale_pool_d, w_out_d, g_final):
    bsz, s_len, _ = x.shape
    v_rows = conv_b_new = conv_c_new = pool_new = None
    for i in range(DEPTH):
        shift, scale, gate = ada_modulation(c, w_ada[i], b_ada[i])
        h = rms_norm(x, g_norm[i]) * (1.0 + scale) + shift
        kind = i % N_MIXERS
        if kind == 0:
            u, v, z = jnp.split(jnp.einsum('bsd,de->bse', h, w_in_a), 3, axis=-1)
            u = jax.nn.gelu(u, approximate=False)
            v = layer_norm(jax.nn.gelu(v, approximate=False), ln_a_g, ln_a_b)
            y = u * chunk_mlp_mix(v, w_s_a, b_s_a)
            out = jnp.einsum('bse,ed->bsd', y * jax.nn.silu(z), w_out_a)
            v_rows = v
        elif kind == 1:
            bg, cg, hv, z = jnp.split(jnp.einsum('bsd,de->bse', h, w_in_b), 4, axis=-1)
            ext, conv_b_new = with_history(cg * hv, hist_b, SCONV_W - 1)
            y = bg * causal_depthwise_conv(ext, w_conv_b)
            out = jnp.einsum('bse,ed->bsd', y * jax.nn.silu(z), w_out_b)
        elif kind == 2:
            a, ag, z = jnp.split(jnp.einsum('bsd,de->bse', h, w_in_c), 3, axis=-1)
            glu = a * jax.nn.sigmoid(ag)
            ext, conv_c_new = with_history(glu, hist_c, CCONV_W - 1)
            y = causal_depthwise_conv(ext, w_conv_c) + b_conv_c
            y = jax.nn.silu(layer_norm(y, ln_c_g, ln_c_b))
            out = jnp.einsum('bse,ed->bsd', y * jax.nn.silu(z), w_out_c)
        else:
            pv, z = jnp.split(jnp.einsum('bsd,de->bse', h, w_in_d), 2, axis=-1)
            ext, pool_new = with_history(pv, hist_d, POOL_HIST)
            dlt = (multi_scale_pool(ext, pos0, s_len) - pv).reshape(bsz, s_len, POOL_GROUPS, POOL_GROUP_W)
            y = jnp.einsum('bsgc,gcf->bsgf', dlt, w_pool_d).reshape(bsz, s_len, D_INNER) * scale_pool_d
            out = jnp.einsum('bse,ed->bsd', y * jax.nn.silu(z), w_out_d)
        x = x + gate * out
    return rms_norm(x, g_final), v_rows, conv_b_new, conv_c_new, pool_new


def setup_inputs(seed: int = 0) -> dict:
    key = jax.random.key(seed)
    ks = jax.random.split(key, 32)
    f32 = jnp.float32
    D, E = D_MODEL, D_INNER

    def nrm(k, shape, scale=1.0):
        return jax.random.normal(k, shape, f32) * scale

    return {
        "x_prompt": nrm(ks[0], (BATCH, SEQ, D)),
        "x_sample": nrm(ks[1], (DEC_BATCH, DEC_SEQ, D)),
        "state_conv_b": nrm(ks[2], (DEC_BATCH, SCONV_W - 1, E)),
        "state_conv_c": nrm(ks[3], (DEC_BATCH, CCONV_W - 1, E), 0.5),
        "state_pool_d": nrm(ks[4], (DEC_BATCH, POOL_HIST, E)),
        "c_prompt": nrm(ks[5], (BATCH, D)),
        "c_sample": nrm(ks[6], (DEC_BATCH, D)),
        "w_ada": nrm(ks[7], (DEPTH, D, 3 * D), 0.5 * D ** -0.5),
        "b_ada": nrm(ks[8], (DEPTH, 3 * D), 0.02),
        "g_norm": 1.0 + nrm(ks[9], (DEPTH, D), 0.02),
        "w_in_a": nrm(ks[10], (D, 3 * E), D ** -0.5),
        "ln_a_g": 1.0 + nrm(ks[11], (E,), 0.02),
        "ln_a_b": nrm(ks[12], (E,), 0.02),
        "w_s_a": nrm(ks[13], (MLP_GROUPS, MLP_BLOCK, MLP_BLOCK), MLP_BLOCK ** -0.5),
        "b_s_a": 1.0 + nrm(ks[14], (MLP_GROUPS, MLP_BLOCK), 0.02),
        "w_out_a": nrm(ks[15], (E, D), E ** -0.5),
        "w_in_b": nrm(ks[16], (D, 4 * E), D ** -0.5),
        "w_conv_b": nrm(ks[17], (SCONV_W, E), SCONV_W ** -0.5),
        "w_out_b": nrm(ks[18], (E, D), E ** -0.5),
        "w_in_c": nrm(ks[19], (D, 3 * E), D ** -0.5),
        "w_conv_c": nrm(ks[20], (CCONV_W, E), CCONV_W ** -0.5),
        "b_conv_c": nrm(ks[21], (E,), 0.02),
        "ln_c_g": 1.0 + nrm(ks[22], (E,), 0.02),
        "ln_c_b": nrm(ks[23], (E,), 0.02),
        "w_out_c": nrm(ks[24], (E, D), E ** -0.5),
        "w_in_d": nrm(ks[25], (D, 2 * E), D ** -0.5),
        "w_pool_d": nrm(ks[26], (POOL_GROUPS, POOL_GROUP_W, POOL_GROUP_W), POOL_GROUP_W ** -0.5),
        "scale_pool_d": 1.0 + nrm(ks[27], (E,), 0.1),
        "w_out_d": nrm(ks[28], (E, D), E ** -0.5),
        "g_final": 1.0 + nrm(ks[29], (D,), 0.02),
    }


def reference(x_prompt, x_sample, state_conv_b, state_conv_c, state_pool_d, c_prompt, c_sample,
              w_ada, b_ada, g_norm,
              w_in_a, ln_a_g, ln_a_b, w_s_a, b_s_a, w_out_a,
              w_in_b, w_conv_b, w_out_b,
              w_in_c, w_conv_c, b_conv_c, ln_c_g, ln_c_b, w_out_c,
              w_in_d, w_pool_d, scale_pool_d, w_out_d, g_final):
    weights = (w_ada, b_ada, g_norm,
               w_in_a, ln_a_g, ln_a_b, w_s_a, b_s_a, w_out_a,
               w_in_b, w_conv_b, w_out_b,
               w_in_c, w_conv_c, b_conv_c, ln_c_g, ln_c_b, w_out_c,
               w_in_d, w_pool_d, scale_pool_d, w_out_d, g_final)
    y_prompt, _, conv_b_prompt, conv_c_prompt, pool_d_prompt = run_trunk(
        x_prompt, c_prompt, None, None, None, 0, *weights)
    y_sample, mlp_v_sample, conv_b_sample, conv_c_sample, pool_d_sample = run_trunk(
        x_sample, c_sample, state_conv_b, state_conv_c, state_pool_d, PAST_LEN, *weights)
    return (y_prompt, y_sample, mlp_v_sample, conv_b_prompt, conv_b_sample,
            conv_c_prompt, conv_c_sample, pool_d_prompt, pool_d_sample)
```

```python
import functools

import jax
import jax.numpy as jnp
from jax import lax
from jax.experimental import pallas as pl
from jax.experimental.pallas import tpu as pltpu

D_MODEL = 1024
D_INNER = 2048
DEPTH = 4
CHUNK = 64
MLP_BLOCK = 128
MLP_GROUPS = 8
MLP_GROUP_W = D_INNER // MLP_GROUPS
SCONV_W = 3
CCONV_W = 31
POOL_WINDOWS = (2, 4, 8, 16)
POOL_GROUP_W = D_INNER // len(POOL_WINDOWS)
POOL_HIST = max(POOL_WINDOWS) - 1
PAST_LEN = 2048
EPS = 1e-6

F32 = jnp.float32
BF16 = jnp.bfloat16

SUBLANES = 8
LANES = 128
VMEM_LIMIT_BYTES = 58 * 1024 * 1024

ROW_TILE = 256
COL_CHUNK = 512
CONV_ROWS = 64
INV_SQRT2 = 0.7071067811865476


def _round_up(n, m):
    return (n + m - 1) // m * m


def _gelu(x):
    return 0.5 * x * (1.0 + lax.erf(x * INV_SQRT2))


def _silu(x):
    return x * jax.nn.sigmoid(x)


def _dot(a, b):
    return jnp.dot(a, b, preferred_element_type=F32)


def _mod_kernel(c_ref, w_ref, b_ref, o_ref):
    c = c_ref[...]
    a = _silu(c).astype(BF16)
    o_ref[0] = _dot(a, w_ref[0].astype(BF16)) + b_ref[0]


def _modulation(c_all, w_ada, b_ada):
    rows = c_all.shape[0]
    nb = 1024
    return pl.pallas_call(
        _mod_kernel,
        grid=(DEPTH, 3 * D_MODEL // nb),
        in_specs=[
            pl.BlockSpec((rows, D_MODEL), lambda i, j: (0, 0)),
            pl.BlockSpec((1, D_MODEL, nb), lambda i, j: (i, 0, j)),
            pl.BlockSpec((1, 1, nb), lambda i, j: (i, 0, j)),
        ],
        out_specs=pl.BlockSpec((1, rows, nb), lambda i, j: (i, 0, j)),
        out_shape=jax.ShapeDtypeStruct((DEPTH, rows, 3 * D_MODEL), F32),
        compiler_params=pltpu.CompilerParams(
            dimension_semantics=("arbitrary", "arbitrary"), vmem_limit_bytes=VMEM_LIMIT_BYTES),
        name="ada_modulation",
    )(c_all, w_ada, b_ada.reshape(DEPTH, 1, 3 * D_MODEL))


def _prenorm(x_ref, mod_ref, g_ref, h_ref, nseq, seg):
    g = g_ref[...]
    for s in range(nseq):
        rows = slice(s * seg, (s + 1) * seg)
        x = x_ref[rows, :]
        ms = jnp.mean(x * x, axis=-1, keepdims=True)
        y = (x * lax.rsqrt(ms + EPS)) * g
        h = y * (1.0 + mod_ref[s, 1:2, :]) + mod_ref[s, 0:1, :]
        h_ref[rows, :] = h.astype(BF16)


def _finish(x_ref, mod_ref, y_ref, wout_ref, gfin_ref, o_ref, nseq, seg):
    out = _dot(y_ref[...], wout_ref[...])
    for s in range(nseq):
        rows = slice(s * seg, (s + 1) * seg)
        xn = x_ref[rows, :] + mod_ref[s, 2:3, :] * out[rows, :]
        if gfin_ref is not None:
            ms = jnp.mean(xn * xn, axis=-1, keepdims=True)
            xn = (xn * lax.rsqrt(ms + EPS)) * gfin_ref[...]
        o_ref[rows, :] = xn


def _init_history(ext_ref, state_ref, hist, hpad, nseq):
    if state_ref is None:
        @pl.when(pl.program_id(1) == 0)
        def _():
            ext_ref[:, 0:hpad, :] = jnp.zeros((nseq, hpad, D_INNER), F32)
    else:
        for s in range(nseq):
            ext_ref[s, hpad - hist:hpad, :] = state_ref[s]


def _roll_history(ext_ref, state_out_ref, cols, hist, hpad, nseq, seg):
    for s in range(nseq):
        last = ext_ref[s, hpad + seg - hist:hpad + seg, cols]
        state_out_ref[s, :, cols] = last
        ext_ref[s, hpad - hist:hpad, cols] = last


def _layer_a_kernel(nseq, seg, emit_v, has_final, *refs):
    refs = list(refs)
    x_ref, mod_ref, g_ref, wv_ref, wuz_ref, lng_ref, lnb_ref, wmix_ref, bmix_ref, wout_ref = refs[:10]
    refs = refs[10:]
    gfin_ref = refs.pop(0) if has_final else None
    o_ref = refs.pop(0)
    v_out_ref = refs.pop(0) if emit_v else None
    h_ref, vbuf_ref, y_ref = refs
    rows_t = nseq * seg
    nch = D_INNER // COL_CHUNK

    _prenorm(x_ref, mod_ref, g_ref, h_ref, nseq, seg)
    h = h_ref[...]

    shift = s1 = s2 = None
    for j in range(nch):
        cols = slice(j * COL_CHUNK, (j + 1) * COL_CHUNK)
        gv = _gelu(_dot(h, wv_ref[:, cols]))
        vbuf_ref[:, cols] = gv
        if j == 0:
            shift = jnp.mean(gv, axis=-1, keepdims=True)
        dv = gv - shift
        a1 = jnp.sum(dv, axis=-1, keepdims=True)
        a2 = jnp.sum(dv * dv, axis=-1, keepdims=True)
        s1 = a1 if s1 is None else s1 + a1
        s2 = a2 if s2 is None else s2 + a2
    m1 = s1 * (1.0 / D_INNER)
    mu = shift + m1
    var = s2 * (1.0 / D_INNER) - m1 * m1
    rstd = lax.rsqrt(var + EPS)

    gpc = COL_CHUNK // MLP_GROUP_W
    for j in range(nch):
        cols = slice(j * COL_CHUNK, (j + 1) * COL_CHUNK)
        vn = (vbuf_ref[:, cols] - mu) * rstd * lng_ref[:, cols] + lnb_ref[:, cols]
        if emit_v:
            v_out_ref[:, cols] = vn
        vnb = vn.astype(BF16)
        mixed = []
        for gi in range(gpc):
            g = j * gpc + gi
            gcols = slice(gi * MLP_GROUP_W, (gi + 1) * MLP_GROUP_W)
            mixed.append(_dot(wmix_ref[g], vnb[:, gcols]) + bmix_ref[g])
        mixed = jnp.concatenate(mixed, axis=1)
        uz = _dot(h, wuz_ref[:, j * 2 * COL_CHUNK:(j + 1) * 2 * COL_CHUNK])
        u = _gelu(uz[:, :COL_CHUNK])
        z = uz[:, COL_CHUNK:]
        y_ref[:, cols] = (u * mixed * _silu(z)).astype(BF16)

    _finish(x_ref, mod_ref, y_ref, wout_ref, gfin_ref, o_ref, nseq, seg)


def _layer_b_kernel(nseq, seg, has_state, has_final, *refs):
    refs = list(refs)
    x_ref, mod_ref, g_ref = refs[:3]
    refs = refs[3:]
    state_ref = refs.pop(0) if has_state else None
    win_ref, wconv_ref, wout_ref = refs[:3]
    refs = refs[3:]
    gfin_ref = refs.pop(0) if has_final else None
    o_ref, state_out_ref, h_ref, ext_ref, y_ref = refs
    hist, hpad = SCONV_W - 1, SUBLANES
    nch = D_INNER // COL_CHUNK
    cc = COL_CHUNK

    _prenorm(x_ref, mod_ref, g_ref, h_ref, nseq, seg)
    _init_history(ext_ref, state_ref, hist, hpad, nseq)
    h = h_ref[...]

    for j in range(nch):
        cols = slice(j * cc, (j + 1) * cc)
        pr = _dot(h, win_ref[:, j * 4 * cc:(j + 1) * 4 * cc])
        bg, cg, hv, z = pr[:, 0:cc], pr[:, cc:2 * cc], pr[:, 2 * cc:3 * cc], pr[:, 3 * cc:4 * cc]
        p = cg * hv
        gate = bg * _silu(z)
        for s in range(nseq):
            rows = slice(s * seg, (s + 1) * seg)
            ext_ref[s, hpad:hpad + seg, cols] = p[rows, :]
            conv = p[rows, :] * wconv_ref[2:3, cols]
            for k in range(SCONV_W - 1):
                d = SCONV_W - 1 - k
                conv = conv + ext_ref[s, hpad - d:hpad - d + seg, cols] * wconv_ref[k:k + 1, cols]
            y_ref[rows, cols] = (conv * gate[rows, :]).astype(BF16)
        _roll_history(ext_ref, state_out_ref, cols, hist, hpad, nseq, seg)

    _finish(x_ref, mod_ref, y_ref, wout_ref, gfin_ref, o_ref, nseq, seg)


def _layer_c_kernel(nseq, seg, has_state, has_final, *refs):
    refs = list(refs)
    x_ref, mod_ref, g_ref = refs[:3]
    refs = refs[3:]
    state_ref = refs.pop(0) if has_state else None
    wag_ref, wz_ref, wconv_ref, bconv_ref, lng_ref, lnb_ref, wout_ref = refs[:7]
    refs = refs[7:]
    gfin_ref = refs.pop(0) if has_final else None
    o_ref, state_out_ref, h_ref, ext_ref, cbuf_ref, y_ref = refs
    hist, hpad = CCONV_W - 1, _round_up(CCONV_W - 1, SUBLANES)
    nch = D_INNER // COL_CHUNK
    cc = COL_CHUNK
    rb = min(CONV_ROWS, seg)

    _prenorm(x_ref, mod_ref, g_ref, h_ref, nseq, seg)
    _init_history(ext_ref, state_ref, hist, hpad, nseq)
    h = h_ref[...]

    shift = s1 = s2 = None
    for j in range(nch):
        cols = slice(j * cc, (j + 1) * cc)
        pr = _dot(h, wag_ref[:, j * 2 * cc:(j + 1) * 2 * cc])
        glu = pr[:, :cc] * jax.nn.sigmoid(pr[:, cc:])
        for s in range(nseq):
            ext_ref[s, hpad:hpad + seg, cols] = glu[s * seg:(s + 1) * seg, :]
        for s in range(nseq):
            for r0 in range(0, seg, rb):
                for c0 in range(j * cc, (j + 1) * cc, LANES):
                    lanes = slice(c0, c0 + LANES)
                    acc = None
                    for k in range(CCONV_W):
                        lo = hpad - (CCONV_W - 1 - k) + r0
                        term = ext_ref[s, lo:lo + rb, lanes] * wconv_ref[k:k + 1, lanes]
                        acc = term if acc is None else acc + term
                    cbuf_ref[s * seg + r0:s * seg + r0 + rb, lanes] = acc + bconv_ref[:, lanes]
        _roll_history(ext_ref, state_out_ref, cols, hist, hpad, nseq, seg)
        cv = cbuf_ref[:, cols]
        if j == 0:
            shift = jnp.mean(cv, axis=-1, keepdims=True)
        dv = cv - shift
        a1 = jnp.sum(dv, axis=-1, keepdims=True)
        a2 = jnp.sum(dv * dv, axis=-1, keepdims=True)
        s1 = a1 if s1 is None else s1 + a1
        s2 = a2 if s2 is None else s2 + a2
    m1 = s1 * (1.0 / D_INNER)
    mu = shift + m1
    var = s2 * (1.0 / D_INNER) - m1 * m1
    rstd = lax.rsqrt(var + EPS)

    for j in range(nch):
        cols = slice(j * cc, (j + 1) * cc)
        yn = (cbuf_ref[:, cols] - mu) * rstd * lng_ref[:, cols] + lnb_ref[:, cols]
        z = _dot(h, wz_ref[:, cols])
        y_ref[:, cols] = (_silu(yn) * _silu(z)).astype(BF16)

    _finish(x_ref, mod_ref, y_ref, wout_ref, gfin_ref, o_ref, nseq, seg)


def _layer_d_kernel(nseq, seg, has_state, has_final, *refs):
    refs = list(refs)
    x_ref, mod_ref, g_ref = refs[:3]
    refs = refs[3:]
    state_ref = refs.pop(0) if has_state else None
    win_ref, wpool_ref, scale_ref, wout_ref = refs[:4]
    refs = refs[4:]
    gfin_ref = refs.pop(0) if has_final else None
    o_ref, state_out_ref, h_ref, ext_ref, y_ref = refs
    hist, hpad = POOL_HIST, _round_up(POOL_HIST, SUBLANES)
    cc = POOL_GROUP_W

    _prenorm(x_ref, mod_ref, g_ref, h_ref, nseq, seg)
    _init_history(ext_ref, state_ref, hist, hpad, nseq)
    h = h_ref[...]

    row = lax.broadcasted_iota(jnp.int32, (seg, 1), 0)
    pos = row + (PAST_LEN if has_state else pl.program_id(1) * seg)

    for gidx, win in enumerate(POOL_WINDOWS):
        cols = slice(gidx * cc, (gidx + 1) * cc)
        pr = _dot(h, win_ref[:, gidx * 2 * cc:(gidx + 1) * 2 * cc])
        pv, z = pr[:, :cc], pr[:, cc:]
        inv_cnt = 1.0 / jnp.minimum(win, pos + 1).astype(F32)
        dlt = []
        for s in range(nseq):
            rows = slice(s * seg, (s + 1) * seg)
            ext_ref[s, hpad:hpad + seg, cols] = pv[rows, :]
            tot = pv[rows, :]
            for d in range(1, win):
                tot = tot + ext_ref[s, hpad - d:hpad - d + seg, cols]
            dlt.append(tot * inv_cnt - pv[rows, :])
        dlt = dlt[0] if nseq == 1 else jnp.concatenate(dlt, axis=0)
        _roll_history(ext_ref, state_out_ref, cols, hist, hpad, nseq, seg)
        yp = _dot(dlt.astype(BF16), wpool_ref[gidx]) * scale_ref[:, cols]
        y_ref[:, cols] = (yp * _silu(z)).astype(BF16)

    _finish(x_ref, mod_ref, y_ref, wout_ref, gfin_ref, o_ref, nseq, seg)


def _const_spec(arr):
    nd = arr.ndim
    return pl.BlockSpec(arr.shape, lambda b, t, _nd=nd: (0,) * _nd, pipeline_mode=pl.Buffered(1))


def _run_layer(body, name, x, mod, g, state, weights, g_final, hist, hpad, nseq, seg, n_batch_steps,
               n_tiles, extra_scratch, emit_v=False, with_state_out=True):
    rows_t = nseq * seg
    m_rows = x.shape[0]
    n_seq_total = n_batch_steps * nseq
    in_arrays = [x, mod, g]
    in_specs = [
        pl.BlockSpec((rows_t, D_MODEL), lambda b, t: (b * n_tiles + t, 0)),
        pl.BlockSpec((nseq, 3, D_MODEL), lambda b, t: (b, 0, 0)),
        _const_spec(g),
    ]
    if state is not None:
        in_arrays.append(state)
        in_specs.append(_const_spec(state))
    for w in weights:
        in_arrays.append(w)
        in_specs.append(_const_spec(w))
    if g_final is not None:
        in_arrays.append(g_final)
        in_specs.append(_const_spec(g_final))

    out_shapes = [jax.ShapeDtypeStruct((m_rows, D_MODEL), F32)]
    out_specs = [pl.BlockSpec((rows_t, D_MODEL), lambda b, t: (b * n_tiles + t, 0))]
    if with_state_out:
        out_shapes.append(jax.ShapeDtypeStruct((n_seq_total, hist, D_INNER), F32))
        out_specs.append(pl.BlockSpec((nseq, hist, D_INNER), lambda b, t: (b, 0, 0)))
    if emit_v:
        out_shapes.append(jax.ShapeDtypeStruct((m_rows, D_INNER), F32))
        out_specs.append(pl.BlockSpec((rows_t, D_INNER), lambda b, t: (b * n_tiles + t, 0)))

    scratch = [pltpu.VMEM((rows_t, D_MODEL), BF16)]
    if with_state_out:
        scratch.append(pltpu.VMEM((nseq, hpad + seg, D_INNER), F32))
    scratch += extra_scratch
    scratch.append(pltpu.VMEM((rows_t, D_INNER), BF16))

    return pl.pallas_call(
        body,
        grid=(n_batch_steps, n_tiles),
        in_specs=in_specs,
        out_specs=out_specs,
        out_shape=out_shapes,
        scratch_shapes=scratch,
        compiler_params=pltpu.CompilerParams(
            dimension_semantics=("arbitrary", "arbitrary"), vmem_limit_bytes=VMEM_LIMIT_BYTES),
        name=name,
    )(*in_arrays)


def _chunk_interleave(w, n_branch, cc):
    d = w.shape[0]
    nch = D_INNER // cc
    return w.reshape(d, n_branch, nch, cc).transpose(0, 2, 1, 3).reshape(d, n_branch * D_INNER)


def _mix_matrices(w_s_a, b_s_a, nseq, seg):
    blk = MLP_BLOCK if seg >= MLP_BLOCK else seg
    cidx = jnp.arange(blk) // CHUNK
    mask = cidx[:, None] >= cidx[None, :]
    w = jnp.where(mask[None], w_s_a[:, :blk, :blk], jnp.zeros((), w_s_a.dtype))
    reps = nseq * seg // blk
    eye = jnp.eye(reps, dtype=w.dtype)
    wmix = jnp.einsum('rs,gij->grisj', eye, w).reshape(MLP_GROUPS, reps * blk, reps * blk)
    bmix = jnp.tile(b_s_a[:, :blk], (1, reps))[:, :, None]
    return wmix.astype(BF16), bmix.astype(F32)


def _trunk(x, mod, state_b, state_c, state_d, nseq, seg, n_batch_steps, n_tiles, emit_v, prm):
    has_state = state_b is not None
    rows_t = nseq * seg
    row = lambda v: v.reshape(1, -1)
    wmix, bmix = _mix_matrices(prm["w_s_a"], prm["b_s_a"], nseq, seg)
    common = dict(nseq=nseq, seg=seg, n_batch_steps=n_batch_steps, n_tiles=n_tiles)

    res = _run_layer(
        functools.partial(_layer_a_kernel, nseq, seg, emit_v, False), "layer_a",
        x, mod[0], row(prm["g_norm"][0]), None,
        [prm["wa_v"], prm["wa_uz"], row(prm["ln_a_g"]), row(prm["ln_a_b"]), wmix, bmix, prm["wa_out"]],
        None, 0, 0, extra_scratch=[pltpu.VMEM((rows_t, D_INNER), F32)], emit_v=emit_v,
        with_state_out=False, **common)
    x = res[0]
    v_rows = res[1] if emit_v else None

    hist, hpad = SCONV_W - 1, SUBLANES
    x, conv_b = _run_layer(
        functools.partial(_layer_b_kernel, nseq, seg, has_state, False), "layer_b",
        x, mod[1], row(prm["g_norm"][1]), state_b,
        [prm["wb_in"], prm["w_conv_b"], prm["wb_out"]],
        None, hist, hpad, extra_scratch=[], **common)

    hist, hpad = CCONV_W - 1, _round_up(CCONV_W - 1, SUBLANES)
    x, conv_c = _run_layer(
        functools.partial(_layer_c_kernel, nseq, seg, has_state, False), "layer_c",
        x, mod[2], row(prm["g_norm"][2]), state_c,
        [prm["wc_ag"], prm["wc_z"], prm["w_conv_c"], row(prm["b_conv_c"]), row(prm["ln_c_g"]),
         row(prm["ln_c_b"]), prm["wc_out"]],
        None, hist, hpad, extra_scratch=[pltpu.VMEM((rows_t, D_INNER), F32)], **common)

    hist, hpad = POOL_HIST, _round_up(POOL_HIST, SUBLANES)
    y, pool_d = _run_layer(
        functools.partial(_layer_d_kernel, nseq, seg, has_state, True), "layer_d",
        x, mod[3], row(prm["g_norm"][3]), state_d,
        [prm["wd_in"], prm["wd_pool"], row(prm["scale_pool_d"]), prm["wd_out"]],
        row(prm["g_final"]), hist, hpad, extra_scratch=[], **common)
    return y, v_rows, conv_b, conv_c, pool_d


def kernel(x_prompt, x_sample, state_conv_b, state_conv_c, state_pool_d, c_prompt, c_sample, w_ada, b_ada, g_norm, w_in_a, ln_a_g, ln_a_b, w_s_a, b_s_a, w_out_a, w_in_b, w_conv_b, w_out_b, w_in_c, w_conv_c, b_conv_c, ln_c_g, ln_c_b, w_out_c, w_in_d, w_pool_d, scale_pool_d, w_out_d, g_final):
    n_p, s_p, _ = x_prompt.shape
    n_s, s_s, _ = x_sample.shape
    e = D_INNER

    c_all = jnp.concatenate([c_prompt, c_sample], axis=0)
    pad = _round_up(n_p + n_s, SUBLANES) - (n_p + n_s)
    c_all = jnp.pad(c_all, ((0, pad), (0, 0)))
    mod = _modulation(c_all, w_ada, b_ada)
    mod_p = mod[:, :n_p].reshape(DEPTH, n_p, 3, D_MODEL)
    mod_s = mod[:, n_p:n_p + n_s].reshape(DEPTH, n_s, 3, D_MODEL)

    prm = dict(
        g_norm=g_norm, ln_a_g=ln_a_g, ln_a_b=ln_a_b, w_s_a=w_s_a, b_s_a=b_s_a,
        w_conv_b=w_conv_b, w_conv_c=w_conv_c, b_conv_c=b_conv_c, ln_c_g=ln_c_g, ln_c_b=ln_c_b,
        scale_pool_d=scale_pool_d, g_final=g_final,
        wa_v=w_in_a[:, e:2 * e].astype(BF16),
        wa_uz=_chunk_interleave(
            jnp.concatenate([w_in_a[:, :e], w_in_a[:, 2 * e:]], axis=1), 2, COL_CHUNK).astype(BF16),
        wa_out=w_out_a.astype(BF16),
        wb_in=_chunk_interleave(w_in_b, 4, COL_CHUNK).astype(BF16),
        wb_out=w_out_b.astype(BF16),
        wc_ag=_chunk_interleave(w_in_c[:, :2 * e], 2, COL_CHUNK).astype(BF16),
        wc_z=w_in_c[:, 2 * e:].astype(BF16),
        wc_out=w_out_c.astype(BF16),
        wd_in=_chunk_interleave(w_in_d, 2, POOL_GROUP_W).astype(BF16),
        wd_pool=w_pool_d.astype(BF16),
        wd_out=w_out_d.astype(BF16),
    )

    y_p, _, cb_p, cc_p, pd_p = _trunk(
        x_prompt.reshape(n_p * s_p, D_MODEL), mod_p, None, None, None,
        nseq=1, seg=ROW_TILE, n_batch_steps=n_p, n_tiles=s_p // ROW_TILE, emit_v=False, prm=prm)
    y_s, v_s, cb_s, cc_s, pd_s = _trunk(
        x_sample.reshape(n_s * s_s, D_MODEL), mod_s, state_conv_b, state_conv_c, state_pool_d,
        nseq=n_s, seg=s_s, n_batch_steps=1, n_tiles=1, emit_v=True, prm=prm)

    return (y_p.reshape(n_p, s_p, D_MODEL), y_s.reshape(n_s, s_s, D_MODEL),
            v_s.reshape(n_s, s_s, D_INNER), cb_p, cb_s, cc_p, cc_s, pd_p, pd_s)
```

```python
import functools

import jax
import jax.numpy as jnp
from jax import lax
from jax.experimental import pallas as pl
from jax.experimental.pallas import tpu as pltpu

D_MODEL = 1024
D_INNER = 2048
DEPTH = 4
CHUNK = 64
MLP_BLOCK = 128
MLP_GROUPS = 8
MLP_GROUP_W = D_INNER // MLP_GROUPS
SCONV_W = 3
CCONV_W = 31
POOL_WINDOWS = (2, 4, 8, 16)
POOL_GROUP_W = D_INNER // len(POOL_WINDOWS)
POOL_HIST = max(POOL_WINDOWS) - 1
PAST_LEN = 2048
EPS = 1e-6

F32 = jnp.float32
BF16 = jnp.bfloat16

SUBLANES = 8
LANES = 128
VMEM_LIMIT_BYTES = 58 * 1024 * 1024

ROW_TILE = 256
COL_CHUNK = 512
CONV_ROWS = 64
INV_SQRT2 = 0.7071067811865476


def _round_up(n, m):
    return (n + m - 1) // m * m


def _gelu(x):
    return 0.5 * x * (1.0 + lax.erf(x * INV_SQRT2))


def _silu(x):
    return x * jax.nn.sigmoid(x)


def _dot(a, b):
    return jnp.dot(a, b, preferred_element_type=F32)


def _mod_kernel(c_ref, w_ref, b_ref, o_ref):
    c = c_ref[...]
    a = _silu(c).astype(BF16)
    o_ref[0] = _dot(a, w_ref[0].astype(BF16)) + b_ref[0]


def _modulation(c_all, w_ada, b_ada):
    rows = c_all.shape[0]
    nb = 1024
    return pl.pallas_call(
        _mod_kernel,
        grid=(DEPTH, 3 * D_MODEL // nb),
        in_specs=[
            pl.BlockSpec((rows, D_MODEL), lambda i, j: (0, 0)),
            pl.BlockSpec((1, D_MODEL, nb), lambda i, j: (i, 0, j)),
            pl.BlockSpec((1, 1, nb), lambda i, j: (i, 0, j)),
        ],
        out_specs=pl.BlockSpec((1, rows, nb), lambda i, j: (i, 0, j)),
        out_shape=jax.ShapeDtypeStruct((DEPTH, rows, 3 * D_MODEL), F32),
        compiler_params=pltpu.CompilerParams(
            dimension_semantics=("arbitrary", "arbitrary"), vmem_limit_bytes=VMEM_LIMIT_BYTES),
        name="ada_modulation",
    )(c_all, w_ada, b_ada.reshape(DEPTH, 1, 3 * D_MODEL))


def _prenorm(x_ref, mod_ref, g_ref, h_ref, nseq, seg):
    g = g_ref[...]
    for s in range(nseq):
        rows = slice(s * seg, (s + 1) * seg)
        x = x_ref[rows, :]
        ms = jnp.mean(x * x, axis=-1, keepdims=True)
        y = (x * lax.rsqrt(ms + EPS)) * g
        h = y * (1.0 + mod_ref[s, 1:2, :]) + mod_ref[s, 0:1, :]
        h_ref[rows, :] = h.astype(BF16)


def _finish(x_ref, mod_ref, y_ref, wout_ref, gfin_ref, o_ref, nseq, seg):
    out = _dot(y_ref[...], wout_ref[...])
    for s in range(nseq):
        rows = slice(s * seg, (s + 1) * seg)
        xn = x_ref[rows, :] + mod_ref[s, 2:3, :] * out[rows, :]
        if gfin_ref is not None:
            ms = jnp.mean(xn * xn, axis=-1, keepdims=True)
            xn = (xn * lax.rsqrt(ms + EPS)) * gfin_ref[...]
        o_ref[rows, :] = xn


def _init_history(ext_ref, state_ref, hist, hpad, nseq):
    if state_ref is None:
        @pl.when(pl.program_id(1) == 0)
        def _():
            ext_ref[:, 0:hpad, :] = jnp.zeros((nseq, hpad, D_INNER), F32)
    else:
        for s in range(nseq):
            ext_ref[s, hpad - hist:hpad, :] = state_ref[s]


def _roll_history(ext_ref, state_out_ref, cols, hist, hpad, nseq, seg):
    for s in range(nseq):
        last = ext_ref[s, hpad + seg - hist:hpad + seg, cols]
        state_out_ref[s, :, cols] = last
        ext_ref[s, hpad - hist:hpad, cols] = last


def _layer_a_kernel(nseq, seg, emit_v, has_final, *refs):
    refs = list(refs)
    x_ref, mod_ref, g_ref, win_ref, lng_ref, lnb_ref, wmix_ref, bmix_ref, wout_ref = refs[:9]
    refs = refs[9:]
    gfin_ref = refs.pop(0) if has_final else None
    o_ref = refs.pop(0)
    v_out_ref = refs.pop(0) if emit_v else None
    h_ref, vbuf_ref, y_ref = refs
    rows_t = nseq * seg
    nch = D_INNER // COL_CHUNK

    _prenorm(x_ref, mod_ref, g_ref, h_ref, nseq, seg)
    h = h_ref[...]

    shift = s1 = s2 = None
    for j in range(nch):
        cols = slice(j * COL_CHUNK, (j + 1) * COL_CHUNK)
        gv = _gelu(_dot(h, win_ref[:, D_INNER + j * COL_CHUNK:D_INNER + (j + 1) * COL_CHUNK]))
        vbuf_ref[:, cols] = gv
        if j == 0:
            shift = jnp.mean(gv, axis=-1, keepdims=True)
        dv = gv - shift
        a1 = jnp.sum(dv, axis=-1, keepdims=True)
        a2 = jnp.sum(dv * dv, axis=-1, keepdims=True)
        s1 = a1 if s1 is None else s1 + a1
        s2 = a2 if s2 is None else s2 + a2
    m1 = s1 * (1.0 / D_INNER)
    mu = shift + m1
    var = s2 * (1.0 / D_INNER) - m1 * m1
    rstd = lax.rsqrt(var + EPS)

    gpc = COL_CHUNK // MLP_GROUP_W
    for j in range(nch):
        cols = slice(j * COL_CHUNK, (j + 1) * COL_CHUNK)
        vn = (vbuf_ref[:, cols] - mu) * rstd * lng_ref[:, cols] + lnb_ref[:, cols]
        if emit_v:
            v_out_ref[:, cols] = vn
        vnb = vn.astype(BF16)
        mixed = []
        for gi in range(gpc):
            g = j * gpc + gi
            gcols = slice(gi * MLP_GROUP_W, (gi + 1) * MLP_GROUP_W)
            mixed.append(_dot(wmix_ref[g], vnb[:, gcols]) + bmix_ref[g])
        mixed = jnp.concatenate(mixed, axis=1)
        u = _gelu(_dot(h, win_ref[:, cols]))
        z = _dot(h, win_ref[:, 2 * D_INNER + j * COL_CHUNK:2 * D_INNER + (j + 1) * COL_CHUNK])
        y_ref[:, cols] = (u * mixed * _silu(z)).astype(BF16)

    _finish(x_ref, mod_ref, y_ref, wout_ref, gfin_ref, o_ref, nseq, seg)


def _layer_b_kernel(nseq, seg, has_state, has_final, *refs):
    refs = list(refs)
    x_ref, mod_ref, g_ref = refs[:3]
    refs = refs[3:]
    state_ref = refs.pop(0) if has_state else None
    win_ref, wconv_ref, wout_ref = refs[:3]
    refs = refs[3:]
    gfin_ref = refs.pop(0) if has_final else None
    o_ref, state_out_ref, h_ref, ext_ref, y_ref = refs
    hist, hpad = SCONV_W - 1, SUBLANES
    nch = D_INNER // COL_CHUNK
    cc = COL_CHUNK

    _prenorm(x_ref, mod_ref, g_ref, h_ref, nseq, seg)
    _init_history(ext_ref, state_ref, hist, hpad, nseq)
    h = h_ref[...]

    for j in range(nch):
        cols = slice(j * cc, (j + 1) * cc)
        bg, cg, hv, z = [
            _dot(h, win_ref[:, br * D_INNER + j * cc:br * D_INNER + (j + 1) * cc]) for br in range(4)]
        p = cg * hv
        gate = bg * _silu(z)
        for s in range(nseq):
            rows = slice(s * seg, (s + 1) * seg)
            ext_ref[s, hpad:hpad + seg, cols] = p[rows, :]
            conv = p[rows, :] * wconv_ref[2:3, cols]
            for k in range(SCONV_W - 1):
                d = SCONV_W - 1 - k
                conv = conv + ext_ref[s, hpad - d:hpad - d + seg, cols] * wconv_ref[k:k + 1, cols]
            y_ref[rows, cols] = (conv * gate[rows, :]).astype(BF16)
        _roll_history(ext_ref, state_out_ref, cols, hist, hpad, nseq, seg)

    _finish(x_ref, mod_ref, y_ref, wout_ref, gfin_ref, o_ref, nseq, seg)


def _layer_c_kernel(nseq, seg, has_state, has_final, *refs):
    refs = list(refs)
    x_ref, mod_ref, g_ref = refs[:3]
    refs = refs[3:]
    state_ref = refs.pop(0) if has_state else None
    win_ref, wconv_ref, bconv_ref, lng_ref, lnb_ref, wout_ref = refs[:6]
    refs = refs[6:]
    gfin_ref = refs.pop(0) if has_final else None
    o_ref, state_out_ref, h_ref, ext_ref, cbuf_ref, sh_ref, y_ref = refs
    hist, hpad = CCONV_W - 1, _round_up(CCONV_W - 1, SUBLANES)
    nch = D_INNER // COL_CHUNK
    cc = COL_CHUNK
    rb = min(CONV_ROWS, seg)

    _prenorm(x_ref, mod_ref, g_ref, h_ref, nseq, seg)
    _init_history(ext_ref, state_ref, hist, hpad, nseq)
    h = h_ref[...]

    shift = s1 = s2 = None
    for j in range(nch):
        cols = slice(j * cc, (j + 1) * cc)
        glu = _dot(h, win_ref[:, cols]) * jax.nn.sigmoid(
            _dot(h, win_ref[:, D_INNER + j * cc:D_INNER + (j + 1) * cc]))
        for s in range(nseq):
            ext_ref[s, hpad:hpad + seg, cols] = glu[s * seg:(s + 1) * seg, :]
        for s in range(nseq):
            e = ext_ref[s, :, cols]
            for q in range(1, SUBLANES):
                sh_ref[q - 1] = pltpu.roll(e, q, axis=0)
            for r0 in range(0, seg, rb):
                for c0 in range(0, cc, LANES):
                    acc = None
                    for k in range(CCONV_W):
                        m, q = divmod(CCONV_W - 1 - k, SUBLANES)
                        lo = hpad + r0 - m * SUBLANES
                        if q == 0:
                            tap = ext_ref[s, lo:lo + rb, j * cc + c0:j * cc + c0 + LANES]
                        else:
                            tap = sh_ref[q - 1, lo:lo + rb, c0:c0 + LANES]
                        term = tap * wconv_ref[k:k + 1, j * cc + c0:j * cc + c0 + LANES]
                        acc = term if acc is None else acc + term
                    cbuf_ref[s * seg + r0:s * seg + r0 + rb, j * cc + c0:j * cc + c0 + LANES] = (
                        acc + bconv_ref[:, j * cc + c0:j * cc + c0 + LANES])
        _roll_history(ext_ref, state_out_ref, cols, hist, hpad, nseq, seg)
        cv = cbuf_ref[:, cols]
        if j == 0:
            shift = jnp.mean(cv, axis=-1, keepdims=True)
        dv = cv - shift
        a1 = jnp.sum(dv, axis=-1, keepdims=True)
        a2 = jnp.sum(dv * dv, axis=-1, keepdims=True)
        s1 = a1 if s1 is None else s1 + a1
        s2 = a2 if s2 is None else s2 + a2
    m1 = s1 * (1.0 / D_INNER)
    mu = shift + m1
    var = s2 * (1.0 / D_INNER) - m1 * m1
    rstd = lax.rsqrt(var + EPS)

    for j in range(nch):
        cols = slice(j * cc, (j + 1) * cc)
        yn = (cbuf_ref[:, cols] - mu) * rstd * lng_ref[:, cols] + lnb_ref[:, cols]
        z = _dot(h, win_ref[:, 2 * D_INNER + j * cc:2 * D_INNER + (j + 1) * cc])
        y_ref[:, cols] = (_silu(yn) * _silu(z)).astype(BF16)

    _finish(x_ref, mod_ref, y_ref, wout_ref, gfin_ref, o_ref, nseq, seg)


def _layer_d_kernel(nseq, seg, has_state, has_final, *refs):
    refs = list(refs)
    x_ref, mod_ref, g_ref = refs[:3]
    refs = refs[3:]
    state_ref = refs.pop(0) if has_state else None
    win_ref, wpool_ref, scale_ref, wout_ref = refs[:4]
    refs = refs[4:]
    gfin_ref = refs.pop(0) if has_final else None
    o_ref, state_out_ref, h_ref, ext_ref, y_ref = refs
    hist, hpad = POOL_HIST, _round_up(POOL_HIST, SUBLANES)
    cc = POOL_GROUP_W

    _prenorm(x_ref, mod_ref, g_ref, h_ref, nseq, seg)
    _init_history(ext_ref, state_ref, hist, hpad, nseq)
    h = h_ref[...]

    row = lax.broadcasted_iota(jnp.int32, (seg, 1), 0)
    pos = row + (PAST_LEN if has_state else pl.program_id(1) * seg)

    for gidx, win in enumerate(POOL_WINDOWS):
        cols = slice(gidx * cc, (gidx + 1) * cc)
        pv = _dot(h, win_ref[:, cols])
        z = _dot(h, win_ref[:, D_INNER + gidx * cc:D_INNER + (gidx + 1) * cc])
        inv_cnt = 1.0 / jnp.minimum(win, pos + 1).astype(F32)
        dlt = []
        for s in range(nseq):
            rows = slice(s * seg, (s + 1) * seg)
            ext_ref[s, hpad:hpad + seg, cols] = pv[rows, :]
            tot = pv[rows, :]
            for d in range(1, win):
                tot = tot + ext_ref[s, hpad - d:hpad - d + seg, cols]
            dlt.append(tot * inv_cnt - pv[rows, :])
        dlt = dlt[0] if nseq == 1 else jnp.concatenate(dlt, axis=0)
        _roll_history(ext_ref, state_out_ref, cols, hist, hpad, nseq, seg)
        yp = _dot(dlt.astype(BF16), wpool_ref[gidx]) * scale_ref[:, cols]
        y_ref[:, cols] = (yp * _silu(z)).astype(BF16)

    _finish(x_ref, mod_ref, y_ref, wout_ref, gfin_ref, o_ref, nseq, seg)


def _const_spec(arr):
    nd = arr.ndim
    return pl.BlockSpec(arr.shape, lambda b, t, _nd=nd: (0,) * _nd, pipeline_mode=pl.Buffered(1))


def _run_layer(body, name, x, mod, g, state, weights, g_final, hist, hpad, nseq, seg, n_batch_steps,
               n_tiles, extra_scratch, emit_v=False, with_state_out=True):
    rows_t = nseq * seg
    m_rows = x.shape[0]
    n_seq_total = n_batch_steps * nseq
    in_arrays = [x, mod, g]
    in_specs = [
        pl.BlockSpec((rows_t, D_MODEL), lambda b, t: (b * n_tiles + t, 0)),
        pl.BlockSpec((nseq, 3, D_MODEL), lambda b, t: (b, 0, 0)),
        _const_spec(g),
    ]
    if state is not None:
        in_arrays.append(state)
        in_specs.append(_const_spec(state))
    for w in weights:
        in_arrays.append(w)
        in_specs.append(_const_spec(w))
    if g_final is not None:
        in_arrays.append(g_final)
        in_specs.append(_const_spec(g_final))

    out_shapes = [jax.ShapeDtypeStruct((m_rows, D_MODEL), F32)]
    out_specs = [pl.BlockSpec((rows_t, D_MODEL), lambda b, t: (b * n_tiles + t, 0))]
    if with_state_out:
        out_shapes.append(jax.ShapeDtypeStruct((n_seq_total, hist, D_INNER), F32))
        out_specs.append(pl.BlockSpec((nseq, hist, D_INNER), lambda b, t: (b, 0, 0)))
    if emit_v:
        out_shapes.append(jax.ShapeDtypeStruct((m_rows, D_INNER), F32))
        out_specs.append(pl.BlockSpec((rows_t, D_INNER), lambda b, t: (b * n_tiles + t, 0)))

    scratch = [pltpu.VMEM((rows_t, D_MODEL), BF16)]
    if with_state_out:
        scratch.append(pltpu.VMEM((nseq, hpad + seg, D_INNER), F32))
    scratch += extra_scratch
    scratch.append(pltpu.VMEM((rows_t, D_INNER), BF16))

    return pl.pallas_call(
        body,
        grid=(n_batch_steps, n_tiles),
        in_specs=in_specs,
        out_specs=out_specs,
        out_shape=out_shapes,
        scratch_shapes=scratch,
        compiler_params=pltpu.CompilerParams(
            dimension_semantics=("arbitrary", "arbitrary"), vmem_limit_bytes=VMEM_LIMIT_BYTES),
        name=name,
    )(*in_arrays)


def _mix_matrices(w_s_a, b_s_a, nseq, seg):
    blk = MLP_BLOCK if seg >= MLP_BLOCK else seg
    cidx = jnp.arange(blk) // CHUNK
    mask = cidx[:, None] >= cidx[None, :]
    w = jnp.where(mask[None], w_s_a[:, :blk, :blk], jnp.zeros((), w_s_a.dtype))
    reps = nseq * seg // blk
    eye = jnp.eye(reps, dtype=w.dtype)
    wmix = jnp.einsum('rs,gij->grisj', eye, w).reshape(MLP_GROUPS, reps * blk, reps * blk)
    bmix = jnp.tile(b_s_a[:, :blk], (1, reps))[:, :, None]
    return wmix.astype(BF16), bmix.astype(F32)


def _trunk(x, mod, state_b, state_c, state_d, nseq, seg, n_batch_steps, n_tiles, emit_v, prm):
    has_state = state_b is not None
    rows_t = nseq * seg
    row = lambda v: v.reshape(1, -1)
    wmix, bmix = _mix_matrices(prm["w_s_a"], prm["b_s_a"], nseq, seg)
    common = dict(nseq=nseq, seg=seg, n_batch_steps=n_batch_steps, n_tiles=n_tiles)

    res = _run_layer(
        functools.partial(_layer_a_kernel, nseq, seg, emit_v, False), "layer_a",
        x, mod[0], row(prm["g_norm"][0]), None,
        [prm["wa_in"], row(prm["ln_a_g"]), row(prm["ln_a_b"]), wmix, bmix, prm["wa_out"]],
        None, 0, 0, extra_scratch=[pltpu.VMEM((rows_t, D_INNER), F32)], emit_v=emit_v,
        with_state_out=False, **common)
    x = res[0]
    v_rows = res[1] if emit_v else None

    hist, hpad = SCONV_W - 1, SUBLANES
    x, conv_b = _run_layer(
        functools.partial(_layer_b_kernel, nseq, seg, has_state, False), "layer_b",
        x, mod[1], row(prm["g_norm"][1]), state_b,
        [prm["wb_in"], prm["w_conv_b"], prm["wb_out"]],
        None, hist, hpad, extra_scratch=[], **common)

    hist, hpad = CCONV_W - 1, _round_up(CCONV_W - 1, SUBLANES)
    x, conv_c = _run_layer(
        functools.partial(_layer_c_kernel, nseq, seg, has_state, False), "layer_c",
        x, mod[2], row(prm["g_norm"][2]), state_c,
        [prm["wc_in"], prm["w_conv_c"], row(prm["b_conv_c"]), row(prm["ln_c_g"]),
         row(prm["ln_c_b"]), prm["wc_out"]],
        None, hist, hpad,
        extra_scratch=[pltpu.VMEM((rows_t, D_INNER), F32),
                       pltpu.VMEM((SUBLANES - 1, hpad + seg, COL_CHUNK), F32)], **common)

    hist, hpad = POOL_HIST, _round_up(POOL_HIST, SUBLANES)
    y, pool_d = _run_layer(
        functools.partial(_layer_d_kernel, nseq, seg, has_state, True), "layer_d",
        x, mod[3], row(prm["g_norm"][3]), state_d,
        [prm["wd_in"], prm["wd_pool"], row(prm["scale_pool_d"]), prm["wd_out"]],
        row(prm["g_final"]), hist, hpad, extra_scratch=[], **common)
    return y, v_rows, conv_b, conv_c, pool_d


def kernel(x_prompt, x_sample, state_conv_b, state_conv_c, state_pool_d, c_prompt, c_sample, w_ada, b_ada, g_norm, w_in_a, ln_a_g, ln_a_b, w_s_a, b_s_a, w_out_a, w_in_b, w_conv_b, w_out_b, w_in_c, w_conv_c, b_conv_c, ln_c_g, ln_c_b, w_out_c, w_in_d, w_pool_d, scale_pool_d, w_out_d, g_final):
    n_p, s_p, _ = x_prompt.shape
    n_s, s_s, _ = x_sample.shape

    c_all = jnp.concatenate([c_prompt, c_sample], axis=0)
    pad = _round_up(n_p + n_s, SUBLANES) - (n_p + n_s)
    c_all = jnp.pad(c_all, ((0, pad), (0, 0)))
    mod = _modulation(c_all, w_ada, b_ada)
    mod_p = mod[:, :n_p].reshape(DEPTH, n_p, 3, D_MODEL)
    mod_s = mod[:, n_p:n_p + n_s].reshape(DEPTH, n_s, 3, D_MODEL)

    prm = dict(
        g_norm=g_norm, ln_a_g=ln_a_g, ln_a_b=ln_a_b, w_s_a=w_s_a, b_s_a=b_s_a,
        w_conv_b=w_conv_b, w_conv_c=w_conv_c, b_conv_c=b_conv_c, ln_c_g=ln_c_g, ln_c_b=ln_c_b,
        scale_pool_d=scale_pool_d, g_final=g_final,
        wa_in=w_in_a.astype(BF16),
        wa_out=w_out_a.astype(BF16),
        wb_in=w_in_b.astype(BF16),
        wb_out=w_out_b.astype(BF16),
        wc_in=w_in_c.astype(BF16),
        wc_out=w_out_c.astype(BF16),
        wd_in=w_in_d.astype(BF16),
        wd_pool=w_pool_d.astype(BF16),
        wd_out=w_out_d.astype(BF16),
    )

    y_p, _, cb_p, cc_p, pd_p = _trunk(
        x_prompt.reshape(n_p * s_p, D_MODEL), mod_p, None, None, None,
        nseq=1, seg=ROW_TILE, n_batch_steps=n_p, n_tiles=s_p // ROW_TILE, emit_v=False, prm=prm)
    y_s, v_s, cb_s, cc_s, pd_s = _trunk(
        x_sample.reshape(n_s * s_s, D_MODEL), mod_s, state_conv_b, state_conv_c, state_pool_d,
        nseq=n_s, seg=s_s, n_batch_steps=1, n_tiles=1, emit_v=True, prm=prm)

    return (y_p.reshape(n_p, s_p, D_MODEL), y_s.reshape(n_s, s_s, D_MODEL),
            v_s.reshape(n_s, s_s, D_INNER), cb_p, cb_s, cc_p, cc_s, pd_p, pd_s)
```

```python
import functools

import jax
import jax.numpy as jnp
from jax import lax
from jax.experimental import pallas as pl
from jax.experimental.pallas import tpu as pltpu

D_MODEL = 1024
D_INNER = 2048
DEPTH = 4
CHUNK = 64
MLP_BLOCK = 128
MLP_GROUPS = 8
MLP_GROUP_W = D_INNER // MLP_GROUPS
SCONV_W = 3
CCONV_W = 31
POOL_WINDOWS = (2, 4, 8, 16)
POOL_GROUP_W = D_INNER // len(POOL_WINDOWS)
POOL_HIST = max(POOL_WINDOWS) - 1
PAST_LEN = 2048
EPS = 1e-6

F32 = jnp.float32
BF16 = jnp.bfloat16

SUBLANES = 8
LANES = 128
MXU_COLS = 256
VMEM_LIMIT_BYTES = 58 * 1024 * 1024

ROW_TILE = 256
COL_CHUNK = 512
CONV_GROUP_MAX = 4
INV_SQRT2 = 0.7071067811865476


def _round_up(n, m):
    return (n + m - 1) // m * m


def _gelu(x):
    return 0.5 * x * (1.0 + lax.erf(x * INV_SQRT2))


def _silu(x):
    return x * jax.nn.sigmoid(x)


def _dot(a, b):
    return jnp.dot(a, b, preferred_element_type=F32)


def _mod_kernel(c_ref, w_ref, b_ref, o_ref):
    c = c_ref[...]
    a = _silu(c).astype(BF16)
    o_ref[0] = _dot(a, w_ref[0].astype(BF16)) + b_ref[0]


def _modulation(c_all, w_ada, b_ada):
    rows = c_all.shape[0]
    nb = 1024
    return pl.pallas_call(
        _mod_kernel,
        grid=(DEPTH, 3 * D_MODEL // nb),
        in_specs=[
            pl.BlockSpec((rows, D_MODEL), lambda i, j: (0, 0)),
            pl.BlockSpec((1, D_MODEL, nb), lambda i, j: (i, 0, j)),
            pl.BlockSpec((1, 1, nb), lambda i, j: (i, 0, j)),
        ],
        out_specs=pl.BlockSpec((1, rows, nb), lambda i, j: (i, 0, j)),
        out_shape=jax.ShapeDtypeStruct((DEPTH, rows, 3 * D_MODEL), F32),
        compiler_params=pltpu.CompilerParams(
            dimension_semantics=("arbitrary", "arbitrary"), vmem_limit_bytes=VMEM_LIMIT_BYTES),
        name="ada_modulation",
    )(c_all, w_ada, b_ada.reshape(DEPTH, 1, 3 * D_MODEL))


def _prenorm(x_ref, mod_ref, g_ref, h_ref, nseq, seg):
    g = g_ref[...]
    for s in range(nseq):
        rows = slice(s * seg, (s + 1) * seg)
        x = x_ref[rows, :]
        ms = jnp.mean(x * x, axis=-1, keepdims=True)
        y = (x * lax.rsqrt(ms + EPS)) * g
        h = y * (1.0 + mod_ref[s, 1:2, :]) + mod_ref[s, 0:1, :]
        h_ref[rows, :] = h.astype(BF16)


def _finish(x_ref, mod_ref, out, gfin_ref, o_ref, nseq, seg):
    for s in range(nseq):
        rows = slice(s * seg, (s + 1) * seg)
        xn = x_ref[rows, :] + mod_ref[s, 2:3, :] * out[rows, :]
        if gfin_ref is not None:
            ms = jnp.mean(xn * xn, axis=-1, keepdims=True)
            xn = (xn * lax.rsqrt(ms + EPS)) * gfin_ref[...]
        o_ref[rows, :] = xn


def _init_history(ext_ref, state_ref, hist, hpad, nseq):
    if state_ref is None:
        @pl.when(pl.program_id(1) == 0)
        def _():
            ext_ref[:, 0:hpad, :] = jnp.zeros((nseq, hpad, D_INNER), F32)
    else:
        for s in range(nseq):
            ext_ref[s, hpad - hist:hpad, :] = state_ref[s]


def _init_slab_history(ext_ref, state_ref, hist, hpad, nseq, seg):
    _, n_slab, ext_rows, _ = ext_ref.shape
    tail = ext_rows - hpad - seg
    if state_ref is None:
        @pl.when(pl.program_id(1) == 0)
        def _():
            ext_ref[:, :, 0:hpad, :] = jnp.zeros((nseq, n_slab, hpad, LANES), F32)
            if tail:
                ext_ref[:, :, hpad + seg:ext_rows, :] = jnp.zeros((nseq, n_slab, tail, LANES), F32)
    else:
        for s in range(nseq):
            for c in range(n_slab):
                ext_ref[s, c, hpad - hist:hpad, :] = state_ref[s, :, c * LANES:(c + 1) * LANES]
        if tail:
            ext_ref[:, :, hpad + seg:ext_rows, :] = jnp.zeros((nseq, n_slab, tail, LANES), F32)


def _group_pitch(seg):
    pitch = seg // SUBLANES
    while pitch % 8 == 0:
        pitch += 1
    return pitch


def _roll_history(ext_ref, state_out_ref, cols, hist, hpad, nseq, seg):
    for s in range(nseq):
        last = ext_ref[s, hpad + seg - hist:hpad + seg, cols]
        state_out_ref[s, :, cols] = last
        ext_ref[s, hpad - hist:hpad, cols] = last


def _layer_a_kernel(nseq, seg, emit_v, has_final, *refs):
    refs = list(refs)
    x_ref, mod_ref, g_ref, win_ref, lng_ref, lnb_ref, wmix_ref, bmix_ref, wout_ref = refs[:9]
    refs = refs[9:]
    gfin_ref = refs.pop(0) if has_final else None
    o_ref = refs.pop(0)
    v_out_ref = refs.pop(0) if emit_v else None
    h_ref, vbuf_ref, y_ref = refs
    rows_t = nseq * seg
    nch = D_INNER // COL_CHUNK

    _prenorm(x_ref, mod_ref, g_ref, h_ref, nseq, seg)
    h = h_ref[...]

    shift = s1 = s2 = None
    for j in range(nch):
        cols = slice(j * COL_CHUNK, (j + 1) * COL_CHUNK)
        gv = _gelu(_dot(h, win_ref[:, D_INNER + j * COL_CHUNK:D_INNER + (j + 1) * COL_CHUNK]))
        vbuf_ref[:, cols] = gv
        if j == 0:
            shift = jnp.mean(gv, axis=-1, keepdims=True)
        dv = gv - shift
        a1 = jnp.sum(dv, axis=-1, keepdims=True)
        a2 = jnp.sum(dv * dv, axis=-1, keepdims=True)
        s1 = a1 if s1 is None else s1 + a1
        s2 = a2 if s2 is None else s2 + a2
    m1 = s1 * (1.0 / D_INNER)
    mu = shift + m1
    var = s2 * (1.0 / D_INNER) - m1 * m1
    rstd = lax.rsqrt(var + EPS)

    gpc = COL_CHUNK // MLP_GROUP_W
    for j in range(nch):
        cols = slice(j * COL_CHUNK, (j + 1) * COL_CHUNK)
        vn = (vbuf_ref[:, cols] - mu) * rstd * lng_ref[:, cols] + lnb_ref[:, cols]
        if emit_v:
            v_out_ref[:, cols] = vn
        vnb = vn.astype(BF16)
        mixed = []
        for gi in range(gpc):
            g = j * gpc + gi
            gcols = slice(gi * MLP_GROUP_W, (gi + 1) * MLP_GROUP_W)
            mixed.append(_dot(wmix_ref[g], vnb[:, gcols]) + bmix_ref[g])
        mixed = jnp.concatenate(mixed, axis=1)
        u = _gelu(_dot(h, win_ref[:, cols]))
        z = _dot(h, win_ref[:, 2 * D_INNER + j * COL_CHUNK:2 * D_INNER + (j + 1) * COL_CHUNK])
        y_ref[:, cols] = (u * mixed * _silu(z)).astype(BF16)

    _finish(x_ref, mod_ref, _dot(y_ref[...], wout_ref[...]), gfin_ref, o_ref, nseq, seg)


def _layer_b_kernel(nseq, seg, has_state, has_final, *refs):
    refs = list(refs)
    x_ref, mod_ref, g_ref = refs[:3]
    refs = refs[3:]
    state_ref = refs.pop(0) if has_state else None
    win_ref, wconv_ref, wout_ref = refs[:3]
    refs = refs[3:]
    gfin_ref = refs.pop(0) if has_final else None
    o_ref, state_out_ref, h_ref, ext_ref, y_ref = refs
    hist, hpad = SCONV_W - 1, SUBLANES
    nch = D_INNER // COL_CHUNK
    cc = COL_CHUNK

    _prenorm(x_ref, mod_ref, g_ref, h_ref, nseq, seg)
    _init_history(ext_ref, state_ref, hist, hpad, nseq)
    h = h_ref[...]

    for j in range(nch):
        cols = slice(j * cc, (j + 1) * cc)
        bg, cg, hv, z = [
            _dot(h, win_ref[:, br * D_INNER + j * cc:br * D_INNER + (j + 1) * cc]) for br in range(4)]
        p = cg * hv
        gate = bg * _silu(z)
        for s in range(nseq):
            rows = slice(s * seg, (s + 1) * seg)
            ext_ref[s, hpad:hpad + seg, cols] = p[rows, :]
            conv = p[rows, :] * wconv_ref[2:3, cols]
            for k in range(SCONV_W - 1):
                d = SCONV_W - 1 - k
                conv = conv + ext_ref[s, hpad - d:hpad - d + seg, cols] * wconv_ref[k:k + 1, cols]
            y_ref[rows, cols] = (conv * gate[rows, :]).astype(BF16)
        _roll_history(ext_ref, state_out_ref, cols, hist, hpad, nseq, seg)

    _finish(x_ref, mod_ref, _dot(y_ref[...], wout_ref[...]), gfin_ref, o_ref, nseq, seg)


def _layer_c_kernel(nseq, seg, has_state, has_final, *refs):
    refs = list(refs)
    x_ref, mod_ref, g_ref = refs[:3]
    refs = refs[3:]
    state_ref = refs.pop(0) if has_state else None
    win_ref, wconv_ref, bconv_ref, lng_ref, lnb_ref, wout_ref = refs[:6]
    refs = refs[6:]
    gfin_ref = refs.pop(0) if has_final else None
    o_ref, state_out_ref, h_ref, ext_ref, cbuf_ref, zs_ref, pbuf_ref = refs
    hist, hpad = CCONV_W - 1, _round_up(CCONV_W - 1, SUBLANES)
    nch = D_INNER // COL_CHUNK
    cc = COL_CHUNK
    pitch = _group_pitch(seg)
    vgroup = max(d for d in range(1, CONV_GROUP_MAX + 1) if pitch % d == 0)

    _init_slab_history(ext_ref, state_ref, hist, hpad, nseq, seg)
    _prenorm(x_ref, mod_ref, g_ref, h_ref, nseq, seg)

    h = h_ref[...]

    def proj_piece(branch, j, piece):
        lo = piece * MXU_COLS
        w_lo = branch * D_INNER + j * cc + lo
        pbuf_ref[branch, :, lo:lo + MXU_COLS] = _dot(h, win_ref[:, w_lo:w_lo + MXU_COLS])

    def conv_group(j, s, c, v0):
        lanes = slice(c * LANES, (c + 1) * LANES)
        accs = [jnp.broadcast_to(bconv_ref[:, lanes], (SUBLANES, LANES))] * vgroup
        for k in range(CCONV_W):
            wk = jnp.broadcast_to(wconv_ref[k:k + 1, lanes], (SUBLANES, LANES))
            for i in range(vgroup):
                start = hpad + v0 + i - (CCONV_W - 1 - k)
                accs[i] = accs[i] + ext_ref[s, c, pl.ds(start, SUBLANES, stride=pitch), :] * wk
        for i in range(vgroup):
            cbuf_ref[s, c, pl.ds(v0 + i, SUBLANES, stride=pitch), :] = accs[i]

    def read_conv(j):
        return jnp.concatenate(
            [jnp.concatenate([cbuf_ref[s, j * (cc // LANES) + ci, 0:seg, :] for ci in range(cc // LANES)], axis=1)
             for s in range(nseq)], axis=0)

    shift = s1 = s2 = None
    n_piece = cc // MXU_COLS
    for piece in range(n_piece):
        proj_piece(0, 0, piece)
        proj_piece(1, 0, piece)
    for j in range(nch):
        glu = pbuf_ref[0] * jax.nn.sigmoid(pbuf_ref[1])
        for s in range(nseq):
            for ci in range(cc // LANES):
                ext_ref[s, j * (cc // LANES) + ci, hpad:hpad + seg, :] = (
                    glu[s * seg:(s + 1) * seg, ci * LANES:(ci + 1) * LANES])
        pieces = [(2, j, p) for p in range(n_piece)]
        if j + 1 < nch:
            pieces += [(br, j + 1, p) for br in (0, 1) for p in range(n_piece)]
        groups = [(j, s, j * (cc // LANES) + ci, v0)
                  for s in range(nseq) for ci in range(cc // LANES) for v0 in range(0, pitch, vgroup)]
        every = max(1, len(groups) // len(pieces))
        for gi, grp in enumerate(groups):
            if gi % every == 0 and pieces:
                proj_piece(*pieces.pop(0))
            conv_group(*grp)
        for rest in pieces:
            proj_piece(*rest)
        for s in range(nseq):
            for ci in range(cc // LANES):
                c = j * (cc // LANES) + ci
                last = ext_ref[s, c, hpad + seg - hist:hpad + seg, :]
                state_out_ref[s, :, c * LANES:(c + 1) * LANES] = last
                ext_ref[s, c, hpad - hist:hpad, :] = last
        zs_ref[:, j * cc:(j + 1) * cc] = _silu(pbuf_ref[2])
        cv = read_conv(j)
        if j == 0:
            shift = jnp.mean(cv, axis=-1, keepdims=True)
        dv = cv - shift
        a1 = jnp.sum(dv, axis=-1, keepdims=True)
        a2 = jnp.sum(dv * dv, axis=-1, keepdims=True)
        s1 = a1 if s1 is None else s1 + a1
        s2 = a2 if s2 is None else s2 + a2
    m1 = s1 * (1.0 / D_INNER)
    mu = shift + m1
    var = s2 * (1.0 / D_INNER) - m1 * m1
    rstd = lax.rsqrt(var + EPS)

    out = None
    for j in range(nch):
        cols = slice(j * cc, (j + 1) * cc)
        yn = (read_conv(j) - mu) * rstd * lng_ref[:, cols] + lnb_ref[:, cols]
        part = _dot((_silu(yn) * zs_ref[:, cols]).astype(BF16), wout_ref[cols, :])
        out = part if out is None else out + part

    _finish(x_ref, mod_ref, out, gfin_ref, o_ref, nseq, seg)


def _layer_d_kernel(nseq, seg, has_state, has_final, *refs):
    refs = list(refs)
    x_ref, mod_ref, g_ref = refs[:3]
    refs = refs[3:]
    state_ref = refs.pop(0) if has_state else None
    win_ref, wpool_ref, scale_ref, wout_ref = refs[:4]
    refs = refs[4:]
    gfin_ref = refs.pop(0) if has_final else None
    o_ref, state_out_ref, h_ref, ext_ref, dbuf_ref, y_ref = refs
    hist, hpad = POOL_HIST, _round_up(POOL_HIST, SUBLANES)
    cc = POOL_GROUP_W
    pitch = _group_pitch(seg)
    spg = cc // LANES

    _init_slab_history(ext_ref, state_ref, hist, hpad, nseq, seg)
    _prenorm(x_ref, mod_ref, g_ref, h_ref, nseq, seg)
    h = h_ref[...]

    lane_row = lax.broadcasted_iota(jnp.int32, (SUBLANES, LANES), 0) * pitch
    pos0 = PAST_LEN if has_state else pl.program_id(1) * seg

    for gidx, win in enumerate(POOL_WINDOWS):
        cols = slice(gidx * cc, (gidx + 1) * cc)
        pv = _dot(h, win_ref[:, cols])
        z = _dot(h, win_ref[:, D_INNER + gidx * cc:D_INNER + (gidx + 1) * cc])
        for s in range(nseq):
            for ci in range(spg):
                ext_ref[s, gidx * spg + ci, hpad:hpad + seg, :] = (
                    pv[s * seg:(s + 1) * seg, ci * LANES:(ci + 1) * LANES])
        for s in range(nseq):
            def rows_at(ci, r, s=s):
                return ext_ref[s, gidx * spg + ci, pl.ds(hpad + r, SUBLANES, stride=pitch), :]

            tots = [None] * spg
            for v in range(pitch):
                inv_cnt = 1.0 / jnp.minimum(win, lane_row + (pos0 + v + 1)).astype(F32)
                for ci in range(spg):
                    cur = rows_at(ci, v)
                    if v == 0:
                        tots[ci] = cur
                        for d in range(1, win):
                            tots[ci] = tots[ci] + rows_at(ci, -d)
                    else:
                        tots[ci] = tots[ci] + cur - rows_at(ci, v - win)
                    dbuf_ref[s, ci, pl.ds(v, SUBLANES, stride=pitch), :] = tots[ci] * inv_cnt - cur
        for s in range(nseq):
            for ci in range(spg):
                c = gidx * spg + ci
                last = ext_ref[s, c, hpad + seg - hist:hpad + seg, :]
                state_out_ref[s, :, c * LANES:(c + 1) * LANES] = last
                ext_ref[s, c, hpad - hist:hpad, :] = last
        dlt = jnp.concatenate(
            [jnp.concatenate([dbuf_ref[s, ci, 0:seg, :] for ci in range(spg)], axis=1) for s in range(nseq)],
            axis=0)
        yp = _dot(dlt.astype(BF16), wpool_ref[gidx]) * scale_ref[:, cols]
        y_ref[:, cols] = (yp * _silu(z)).astype(BF16)

    _finish(x_ref, mod_ref, _dot(y_ref[...], wout_ref[...]), gfin_ref, o_ref, nseq, seg)


def _const_spec(arr):
    nd = arr.ndim
    return pl.BlockSpec(arr.shape, lambda b, t, _nd=nd: (0,) * _nd, pipeline_mode=pl.Buffered(1))


def _run_layer(body, name, x, mod, g, state, weights, g_final, hist, hpad, nseq, seg, n_batch_steps,
               n_tiles, extra_scratch, emit_v=False, with_state_out=True, ext_shape=None, stage_y=True):
    rows_t = nseq * seg
    m_rows = x.shape[0]
    n_seq_total = n_batch_steps * nseq
    in_arrays = [x, mod, g]
    in_specs = [
        pl.BlockSpec((rows_t, D_MODEL), lambda b, t: (b * n_tiles + t, 0)),
        pl.BlockSpec((nseq, 3, D_MODEL), lambda b, t: (b, 0, 0)),
        _const_spec(g),
    ]
    if state is not None:
        in_arrays.append(state)
        in_specs.append(_const_spec(state))
    for w in weights:
        in_arrays.append(w)
        in_specs.append(_const_spec(w))
    if g_final is not None:
        in_arrays.append(g_final)
        in_specs.append(_const_spec(g_final))

    out_shapes = [jax.ShapeDtypeStruct((m_rows, D_MODEL), F32)]
    out_specs = [pl.BlockSpec((rows_t, D_MODEL), lambda b, t: (b * n_tiles + t, 0))]
    if with_state_out:
        out_shapes.append(jax.ShapeDtypeStruct((n_seq_total, hist, D_INNER), F32))
        out_specs.append(pl.BlockSpec((nseq, hist, D_INNER), lambda b, t: (b, 0, 0)))
    if emit_v:
        out_shapes.append(jax.ShapeDtypeStruct((m_rows, D_INNER), F32))
        out_specs.append(pl.BlockSpec((rows_t, D_INNER), lambda b, t: (b * n_tiles + t, 0)))

    scratch = [pltpu.VMEM((rows_t, D_MODEL), BF16)]
    if with_state_out:
        scratch.append(pltpu.VMEM(ext_shape or (nseq, hpad + seg, D_INNER), F32))
    scratch += extra_scratch
    if stage_y:
        scratch.append(pltpu.VMEM((rows_t, D_INNER), BF16))

    return pl.pallas_call(
        body,
        grid=(n_batch_steps, n_tiles),
        in_specs=in_specs,
        out_specs=out_specs,
        out_shape=out_shapes,
        scratch_shapes=scratch,
        compiler_params=pltpu.CompilerParams(
            dimension_semantics=("arbitrary", "arbitrary"), vmem_limit_bytes=VMEM_LIMIT_BYTES),
        name=name,
    )(*in_arrays)


def _mix_matrices(w_s_a, b_s_a, nseq, seg):
    blk = MLP_BLOCK if seg >= MLP_BLOCK else seg
    cidx = jnp.arange(blk) // CHUNK
    mask = cidx[:, None] >= cidx[None, :]
    w = jnp.where(mask[None], w_s_a[:, :blk, :blk], jnp.zeros((), w_s_a.dtype))
    reps = nseq * seg // blk
    eye = jnp.eye(reps, dtype=w.dtype)
    wmix = jnp.einsum('rs,gij->grisj', eye, w).reshape(MLP_GROUPS, reps * blk, reps * blk)
    bmix = jnp.tile(b_s_a[:, :blk], (1, reps))[:, :, None]
    return wmix.astype(BF16), bmix.astype(F32)


def _trunk(x, mod, state_b, state_c, state_d, nseq, seg, n_batch_steps, n_tiles, emit_v, prm):
    has_state = state_b is not None
    rows_t = nseq * seg
    row = lambda v: v.reshape(1, -1)
    wmix, bmix = _mix_matrices(prm["w_s_a"], prm["b_s_a"], nseq, seg)
    common = dict(nseq=nseq, seg=seg, n_batch_steps=n_batch_steps, n_tiles=n_tiles)

    res = _run_layer(
        functools.partial(_layer_a_kernel, nseq, seg, emit_v, False), "layer_a",
        x, mod[0], row(prm["g_norm"][0]), None,
        [prm["wa_in"], row(prm["ln_a_g"]), row(prm["ln_a_b"]), wmix, bmix, prm["wa_out"]],
        None, 0, 0, extra_scratch=[pltpu.VMEM((rows_t, D_INNER), F32)], emit_v=emit_v,
        with_state_out=False, **common)
    x = res[0]
    v_rows = res[1] if emit_v else None

    hist, hpad = SCONV_W - 1, SUBLANES
    x, conv_b = _run_layer(
        functools.partial(_layer_b_kernel, nseq, seg, has_state, False), "layer_b",
        x, mod[1], row(prm["g_norm"][1]), state_b,
        [prm["wb_in"], prm["w_conv_b"], prm["wb_out"]],
        None, hist, hpad, extra_scratch=[], **common)

    hist, hpad = CCONV_W - 1, _round_up(CCONV_W - 1, SUBLANES)
    pitch = _group_pitch(seg)
    x, conv_c = _run_layer(
        functools.partial(_layer_c_kernel, nseq, seg, has_state, False), "layer_c",
        x, mod[2], row(prm["g_norm"][2]), state_c,
        [prm["wc_in"], prm["w_conv_c"], row(prm["b_conv_c"]), row(prm["ln_c_g"]),
         row(prm["ln_c_b"]), prm["wc_out"]],
        None, hist, hpad,
        extra_scratch=[pltpu.VMEM((nseq, D_INNER // LANES, SUBLANES * pitch, LANES), F32),
                       pltpu.VMEM((rows_t, D_INNER), F32),
                       pltpu.VMEM((3, rows_t, COL_CHUNK), F32)],
        ext_shape=(nseq, D_INNER // LANES, hpad + SUBLANES * pitch, LANES), stage_y=False, **common)

    hist, hpad = POOL_HIST, _round_up(POOL_HIST, SUBLANES)
    y, pool_d = _run_layer(
        functools.partial(_layer_d_kernel, nseq, seg, has_state, True), "layer_d",
        x, mod[3], row(prm["g_norm"][3]), state_d,
        [prm["wd_in"], prm["wd_pool"], row(prm["scale_pool_d"]), prm["wd_out"]],
        row(prm["g_final"]), hist, hpad,
        extra_scratch=[pltpu.VMEM((nseq, POOL_GROUP_W // LANES, SUBLANES * pitch, LANES), F32)],
        ext_shape=(nseq, D_INNER // LANES, hpad + SUBLANES * pitch, LANES), **common)
    return y, v_rows, conv_b, conv_c, pool_d


def kernel(x_prompt, x_sample, state_conv_b, state_conv_c, state_pool_d, c_prompt, c_sample, w_ada, b_ada, g_norm, w_in_a, ln_a_g, ln_a_b, w_s_a, b_s_a, w_out_a, w_in_b, w_conv_b, w_out_b, w_in_c, w_conv_c, b_conv_c, ln_c_g, ln_c_b, w_out_c, w_in_d, w_pool_d, scale_pool_d, w_out_d, g_final):
    n_p, s_p, _ = x_prompt.shape
    n_s, s_s, _ = x_sample.shape

    c_all = jnp.concatenate([c_prompt, c_sample], axis=0)
    pad = _round_up(n_p + n_s, SUBLANES) - (n_p + n_s)
    c_all = jnp.pad(c_all, ((0, pad), (0, 0)))
    mod = _modulation(c_all, w_ada, b_ada)
    mod_p = mod[:, :n_p].reshape(DEPTH, n_p, 3, D_MODEL)
    mod_s = mod[:, n_p:n_p + n_s].reshape(DEPTH, n_s, 3, D_MODEL)

    prm = dict(
        g_norm=g_norm, ln_a_g=ln_a_g, ln_a_b=ln_a_b, w_s_a=w_s_a, b_s_a=b_s_a,
        w_conv_b=w_conv_b, w_conv_c=w_conv_c, b_conv_c=b_conv_c, ln_c_g=ln_c_g, ln_c_b=ln_c_b,
        scale_pool_d=scale_pool_d, g_final=g_final,
        wa_in=w_in_a.astype(BF16),
        wa_out=w_out_a.astype(BF16),
        wb_in=w_in_b.astype(BF16),
        wb_out=w_out_b.astype(BF16),
        wc_in=w_in_c.astype(BF16),
        wc_out=w_out_c.astype(BF16),
        wd_in=w_in_d.astype(BF16),
        wd_pool=w_pool_d.astype(BF16),
        wd_out=w_out_d.astype(BF16),
    )

    y_p, _, cb_p, cc_p, pd_p = _trunk(
        x_prompt.reshape(n_p * s_p, D_MODEL), mod_p, None, None, None,
        nseq=1, seg=ROW_TILE, n_batch_steps=n_p, n_tiles=s_p // ROW_TILE, emit_v=False, prm=prm)
    y_s, v_s, cb_s, cc_s, pd_s = _trunk(
        x_sample.reshape(n_s * s_s, D_MODEL), mod_s, state_conv_b, state_conv_c, state_pool_d,
        nseq=n_s, seg=s_s, n_batch_steps=1, n_tiles=1, emit_v=True, prm=prm)

    return (y_p.reshape(n_p, s_p, D_MODEL), y_s.reshape(n_s, s_s, D_MODEL),
            v_s.reshape(n_s, s_s, D_INNER), cb_p, cb_s, cc_p, cc_s, pd_p, pd_s)
```

```python
import functools

import jax
import jax.numpy as jnp
from jax import lax
from jax.experimental import pallas as pl
from jax.experimental.pallas import tpu as pltpu

D_MODEL = 1024
D_INNER = 2048
DEPTH = 4
CHUNK = 64
MLP_BLOCK = 128
MLP_GROUPS = 8
MLP_GROUP_W = D_INNER // MLP_GROUPS
SCONV_W = 3
CCONV_W = 31
POOL_WINDOWS = (2, 4, 8, 16)
POOL_GROUP_W = D_INNER // len(POOL_WINDOWS)
POOL_HIST = max(POOL_WINDOWS) - 1
PAST_LEN = 2048
EPS = 1e-6

F32 = jnp.float32
BF16 = jnp.bfloat16

SUBLANES = 8
LANES = 128
MXU_COLS = 256
VMEM_LIMIT_BYTES = 58 * 1024 * 1024

ROW_TILE = 512
SEG_ROWS = 256
MIX_ROWS = 256
COL_CHUNK = 512
CONV_GROUP_MAX = 4
INV_SQRT2 = 0.7071067811865476


def _round_up(n, m):
    return (n + m - 1) // m * m


def _gelu(x):
    return 0.5 * x * (1.0 + lax.erf(x * INV_SQRT2))


def _silu(x):
    return x * jax.nn.sigmoid(x)


def _dot(a, b):
    return jnp.dot(a, b, preferred_element_type=F32)


def _mod_kernel(c_ref, w_ref, b_ref, o_ref):
    c = c_ref[...]
    a = _silu(c).astype(BF16)
    o_ref[0] = _dot(a, w_ref[0].astype(BF16)) + b_ref[0]


def _modulation(c_all, w_ada, b_ada):
    rows = c_all.shape[0]
    nb = 1024
    return pl.pallas_call(
        _mod_kernel,
        grid=(DEPTH, 3 * D_MODEL // nb),
        in_specs=[
            pl.BlockSpec((rows, D_MODEL), lambda i, j: (0, 0)),
            pl.BlockSpec((1, D_MODEL, nb), lambda i, j: (i, 0, j)),
            pl.BlockSpec((1, 1, nb), lambda i, j: (i, 0, j)),
        ],
        out_specs=pl.BlockSpec((1, rows, nb), lambda i, j: (i, 0, j)),
        out_shape=jax.ShapeDtypeStruct((DEPTH, rows, 3 * D_MODEL), F32),
        compiler_params=pltpu.CompilerParams(
            dimension_semantics=("arbitrary", "arbitrary"), vmem_limit_bytes=VMEM_LIMIT_BYTES),
        name="ada_modulation",
    )(c_all, w_ada, b_ada.reshape(DEPTH, 1, 3 * D_MODEL))


def _prenorm(x_ref, mod_ref, g_ref, h_ref, nseq, seg):
    g = g_ref[...]
    for s in range(nseq):
        rows = slice(s * seg, (s + 1) * seg)
        x = x_ref[rows, :]
        ms = jnp.mean(x * x, axis=-1, keepdims=True)
        y = (x * lax.rsqrt(ms + EPS)) * g
        m = s if mod_ref.shape[0] > 1 else 0
        h = y * (1.0 + mod_ref[m, 1:2, :]) + mod_ref[m, 0:1, :]
        h_ref[rows, :] = h.astype(BF16)


def _finish(x_ref, mod_ref, out, gfin_ref, o_ref, nseq, seg):
    for s in range(nseq):
        rows = slice(s * seg, (s + 1) * seg)
        m = s if mod_ref.shape[0] > 1 else 0
        xn = x_ref[rows, :] + mod_ref[m, 2:3, :] * out[rows, :]
        if gfin_ref is not None:
            ms = jnp.mean(xn * xn, axis=-1, keepdims=True)
            xn = (xn * lax.rsqrt(ms + EPS)) * gfin_ref[...]
        o_ref[rows, :] = xn


def _init_history(ext_ref, state_ref, hist, hpad, nseq):
    if state_ref is None:
        @pl.when(pl.program_id(1) == 0)
        def _():
            ext_ref[:, 0:hpad, :] = jnp.zeros((nseq, hpad, D_INNER), F32)
    else:
        for s in range(nseq):
            ext_ref[s, hpad - hist:hpad, :] = state_ref[s]


def _init_slab_history(ext_ref, state_ref, hist, hpad, nseq, seg):
    _, n_slab, ext_rows, _ = ext_ref.shape
    tail = ext_rows - hpad - seg
    if state_ref is None:
        @pl.when(pl.program_id(1) == 0)
        def _():
            ext_ref[:, :, 0:hpad, :] = jnp.zeros((nseq, n_slab, hpad, LANES), F32)
            if tail:
                ext_ref[:, :, hpad + seg:ext_rows, :] = jnp.zeros((nseq, n_slab, tail, LANES), F32)
    else:
        for s in range(nseq):
            for c in range(n_slab):
                ext_ref[s, c, hpad - hist:hpad, :] = state_ref[s, :, c * LANES:(c + 1) * LANES]
        if tail:
            ext_ref[:, :, hpad + seg:ext_rows, :] = jnp.zeros((nseq, n_slab, tail, LANES), F32)


def _slab_rows(c, s, r0, r1):
    return (s, c, slice(r0, r1), slice(None))


def _slab_lanes(c, s):
    return (s, slice(None), slice(c * LANES, (c + 1) * LANES))


def _group_pitch(seg):
    pitch = seg // SUBLANES
    while pitch % 8 == 0:
        pitch += 1
    return pitch


def _link_history(ext_ref, at, hist, hpad, nseq, seg):
    for s in range(1, nseq):
        ext_ref[at(s, hpad - hist, hpad)] = ext_ref[at(s - 1, hpad + seg - hist, hpad + seg)]


def _roll_history(ext_ref, state_out_ref, at, out_at, hist, hpad, nseq, seg, chained):
    if chained:
        last = ext_ref[at(nseq - 1, hpad + seg - hist, hpad + seg)]
        state_out_ref[out_at(0)] = last
        ext_ref[at(0, hpad - hist, hpad)] = last
    else:
        for s in range(nseq):
            last = ext_ref[at(s, hpad + seg - hist, hpad + seg)]
            state_out_ref[out_at(s)] = last
            ext_ref[at(s, hpad - hist, hpad)] = last


def _layer_a_kernel(nseq, seg, emit_v, has_final, *refs):
    refs = list(refs)
    x_ref, mod_ref, g_ref, win_ref, lng_ref, lnb_ref, wmix_ref, bmix_ref, wout_ref = refs[:9]
    refs = refs[9:]
    gfin_ref = refs.pop(0) if has_final else None
    o_ref = refs.pop(0)
    v_out_ref = refs.pop(0) if emit_v else None
    h_ref, vbuf_ref, y_ref = refs
    rows_t = nseq * seg
    mix_rows = wmix_ref.shape[1]
    nch = D_INNER // COL_CHUNK

    _prenorm(x_ref, mod_ref, g_ref, h_ref, nseq, seg)
    h = h_ref[...]

    shift = s1 = s2 = None
    for j in range(nch):
        cols = slice(j * COL_CHUNK, (j + 1) * COL_CHUNK)
        gv = _gelu(_dot(h, win_ref[:, D_INNER + j * COL_CHUNK:D_INNER + (j + 1) * COL_CHUNK]))
        vbuf_ref[:, cols] = gv
        if j == 0:
            shift = jnp.mean(gv, axis=-1, keepdims=True)
        dv = gv - shift
        a1 = jnp.sum(dv, axis=-1, keepdims=True)
        a2 = jnp.sum(dv * dv, axis=-1, keepdims=True)
        s1 = a1 if s1 is None else s1 + a1
        s2 = a2 if s2 is None else s2 + a2
    m1 = s1 * (1.0 / D_INNER)
    mu = shift + m1
    var = s2 * (1.0 / D_INNER) - m1 * m1
    rstd = lax.rsqrt(var + EPS)

    gpc = COL_CHUNK // MLP_GROUP_W
    for j in range(nch):
        cols = slice(j * COL_CHUNK, (j + 1) * COL_CHUNK)
        vn = (vbuf_ref[:, cols] - mu) * rstd * lng_ref[:, cols] + lnb_ref[:, cols]
        if emit_v:
            v_out_ref[:, cols] = vn
        vnb = vn.astype(BF16)
        mixed = []
        for gi in range(gpc):
            g = j * gpc + gi
            gcols = slice(gi * MLP_GROUP_W, (gi + 1) * MLP_GROUP_W)
            mixed.append(jnp.concatenate(
                [_dot(wmix_ref[g], vnb[r0:r0 + mix_rows, gcols]) + bmix_ref[g]
                 for r0 in range(0, rows_t, mix_rows)], axis=0))
        mixed = jnp.concatenate(mixed, axis=1)
        u = _gelu(_dot(h, win_ref[:, cols]))
        z = _dot(h, win_ref[:, 2 * D_INNER + j * COL_CHUNK:2 * D_INNER + (j + 1) * COL_CHUNK])
        y_ref[:, cols] = (u * mixed * _silu(z)).astype(BF16)

    _finish(x_ref, mod_ref, _dot(y_ref[...], wout_ref[...]), gfin_ref, o_ref, nseq, seg)


def _layer_b_kernel(nseq, seg, has_state, has_final, *refs):
    refs = list(refs)
    x_ref, mod_ref, g_ref = refs[:3]
    refs = refs[3:]
    state_ref = refs.pop(0) if has_state else None
    win_ref, wconv_ref, wout_ref = refs[:3]
    refs = refs[3:]
    gfin_ref = refs.pop(0) if has_final else None
    o_ref, state_out_ref, h_ref, ext_ref, y_ref = refs
    hist, hpad = SCONV_W - 1, SUBLANES
    nch = D_INNER // COL_CHUNK
    cc = COL_CHUNK

    _prenorm(x_ref, mod_ref, g_ref, h_ref, nseq, seg)
    _init_history(ext_ref, state_ref, hist, hpad, nseq)
    h = h_ref[...]

    for j in range(nch):
        cols = slice(j * cc, (j + 1) * cc)
        bg, cg, hv, z = [
            _dot(h, win_ref[:, br * D_INNER + j * cc:br * D_INNER + (j + 1) * cc]) for br in range(4)]
        p = cg * hv
        gate = bg * _silu(z)
        at = lambda s, r0, r1, cols=cols: (s, slice(r0, r1), cols)
        for s in range(nseq):
            ext_ref[s, hpad:hpad + seg, cols] = p[s * seg:(s + 1) * seg, :]
        if not has_state:
            _link_history(ext_ref, at, hist, hpad, nseq, seg)
        for s in range(nseq):
            rows = slice(s * seg, (s + 1) * seg)
            conv = p[rows, :] * wconv_ref[2:3, cols]
            for k in range(SCONV_W - 1):
                d = SCONV_W - 1 - k
                conv = conv + ext_ref[s, hpad - d:hpad - d + seg, cols] * wconv_ref[k:k + 1, cols]
            y_ref[rows, cols] = (conv * gate[rows, :]).astype(BF16)
        _roll_history(ext_ref, state_out_ref, at, lambda s, cols=cols: (s, slice(None), cols),
                      hist, hpad, nseq, seg, chained=not has_state)

    _finish(x_ref, mod_ref, _dot(y_ref[...], wout_ref[...]), gfin_ref, o_ref, nseq, seg)


def _layer_c_kernel(nseq, seg, has_state, has_final, *refs):
    refs = list(refs)
    x_ref, mod_ref, g_ref = refs[:3]
    refs = refs[3:]
    state_ref = refs.pop(0) if has_state else None
    win_ref, wconv_ref, bconv_ref, lng_ref, lnb_ref, wout_ref = refs[:6]
    refs = refs[6:]
    gfin_ref = refs.pop(0) if has_final else None
    o_ref, state_out_ref, h_ref, ext_ref, cbuf_ref, zs_ref, pbuf_ref = refs
    hist, hpad = CCONV_W - 1, _round_up(CCONV_W - 1, SUBLANES)
    nch = D_INNER // COL_CHUNK
    cc = COL_CHUNK
    pitch = _group_pitch(seg)
    vgroup = max(d for d in range(1, CONV_GROUP_MAX + 1) if pitch % d == 0)

    _init_slab_history(ext_ref, state_ref, hist, hpad, nseq, seg)
    _prenorm(x_ref, mod_ref, g_ref, h_ref, nseq, seg)

    h = h_ref[...]

    def proj_piece(branch, j, piece):
        lo = piece * MXU_COLS
        w_lo = branch * D_INNER + j * cc + lo
        pbuf_ref[branch, :, lo:lo + MXU_COLS] = _dot(h, win_ref[:, w_lo:w_lo + MXU_COLS])

    def conv_group(j, s, c, v0):
        lanes = slice(c * LANES, (c + 1) * LANES)
        accs = [jnp.broadcast_to(bconv_ref[:, lanes], (SUBLANES, LANES))] * vgroup
        for k in range(CCONV_W):
            wk = jnp.broadcast_to(wconv_ref[k:k + 1, lanes], (SUBLANES, LANES))
            for i in range(vgroup):
                start = hpad + v0 + i - (CCONV_W - 1 - k)
                accs[i] = accs[i] + ext_ref[s, c, pl.ds(start, SUBLANES, stride=pitch), :] * wk
        for i in range(vgroup):
            cbuf_ref[s, c, pl.ds(v0 + i, SUBLANES, stride=pitch), :] = accs[i]

    def read_conv(j):
        return jnp.concatenate(
            [jnp.concatenate([cbuf_ref[s, j * (cc // LANES) + ci, 0:seg, :] for ci in range(cc // LANES)], axis=1)
             for s in range(nseq)], axis=0)

    shift = s1 = s2 = None
    n_piece = cc // MXU_COLS
    for piece in range(n_piece):
        proj_piece(0, 0, piece)
        proj_piece(1, 0, piece)
    for j in range(nch):
        glu = pbuf_ref[0] * jax.nn.sigmoid(pbuf_ref[1])
        for s in range(nseq):
            for ci in range(cc // LANES):
                ext_ref[s, j * (cc // LANES) + ci, hpad:hpad + seg, :] = (
                    glu[s * seg:(s + 1) * seg, ci * LANES:(ci + 1) * LANES])
        if not has_state:
            for ci in range(cc // LANES):
                _link_history(ext_ref, functools.partial(_slab_rows,j * (cc // LANES) + ci), hist, hpad, nseq, seg)
        pieces = [(2, j, p) for p in range(n_piece)]
        if j + 1 < nch:
            pieces += [(br, j + 1, p) for br in (0, 1) for p in range(n_piece)]
        groups = [(j, s, j * (cc // LANES) + ci, v0)
                  for s in range(nseq) for ci in range(cc // LANES) for v0 in range(0, pitch, vgroup)]
        every = max(1, len(groups) // len(pieces))
        for gi, grp in enumerate(groups):
            if gi % every == 0 and pieces:
                proj_piece(*pieces.pop(0))
            conv_group(*grp)
        for rest in pieces:
            proj_piece(*rest)
        for ci in range(cc // LANES):
            c = j * (cc // LANES) + ci
            _roll_history(ext_ref, state_out_ref, functools.partial(_slab_rows,c),
                          functools.partial(_slab_lanes,c), hist, hpad, nseq, seg, chained=not has_state)
        zs_ref[:, j * cc:(j + 1) * cc] = _silu(pbuf_ref[2])
        cv = read_conv(j)
        if j == 0:
            shift = jnp.mean(cv, axis=-1, keepdims=True)
        dv = cv - shift
        a1 = jnp.sum(dv, axis=-1, keepdims=True)
        a2 = jnp.sum(dv * dv, axis=-1, keepdims=True)
        s1 = a1 if s1 is None else s1 + a1
        s2 = a2 if s2 is None else s2 + a2
    m1 = s1 * (1.0 / D_INNER)
    mu = shift + m1
    var = s2 * (1.0 / D_INNER) - m1 * m1
    rstd = lax.rsqrt(var + EPS)

    out = None
    for j in range(nch):
        cols = slice(j * cc, (j + 1) * cc)
        yn = (read_conv(j) - mu) * rstd * lng_ref[:, cols] + lnb_ref[:, cols]
        part = _dot((_silu(yn) * zs_ref[:, cols]).astype(BF16), wout_ref[cols, :])
        out = part if out is None else out + part

    _finish(x_ref, mod_ref, out, gfin_ref, o_ref, nseq, seg)


def _layer_d_kernel(nseq, seg, has_state, has_final, *refs):
    refs = list(refs)
    x_ref, mod_ref, g_ref = refs[:3]
    refs = refs[3:]
    state_ref = refs.pop(0) if has_state else None
    win_ref, wpool_ref, scale_ref, wout_ref = refs[:4]
    refs = refs[4:]
    gfin_ref = refs.pop(0) if has_final else None
    o_ref, state_out_ref, h_ref, ext_ref, dbuf_ref, y_ref = refs
    hist, hpad = POOL_HIST, _round_up(POOL_HIST, SUBLANES)
    cc = POOL_GROUP_W
    pitch = _group_pitch(seg)
    spg = cc // LANES

    _init_slab_history(ext_ref, state_ref, hist, hpad, nseq, seg)
    _prenorm(x_ref, mod_ref, g_ref, h_ref, nseq, seg)
    h = h_ref[...]

    lane_row = lax.broadcasted_iota(jnp.int32, (SUBLANES, LANES), 0) * pitch

    for gidx, win in enumerate(POOL_WINDOWS):
        cols = slice(gidx * cc, (gidx + 1) * cc)
        pv = _dot(h, win_ref[:, cols])
        z = _dot(h, win_ref[:, D_INNER + gidx * cc:D_INNER + (gidx + 1) * cc])
        for s in range(nseq):
            for ci in range(spg):
                ext_ref[s, gidx * spg + ci, hpad:hpad + seg, :] = (
                    pv[s * seg:(s + 1) * seg, ci * LANES:(ci + 1) * LANES])
        if not has_state:
            for ci in range(spg):
                _link_history(ext_ref, functools.partial(_slab_rows, gidx * spg + ci), hist, hpad, nseq, seg)
        for s in range(nseq):
            pos0 = PAST_LEN if has_state else pl.program_id(1) * (nseq * seg) + s * seg

            def rows_at(ci, r, s=s):
                return ext_ref[s, gidx * spg + ci, pl.ds(hpad + r, SUBLANES, stride=pitch), :]

            tots = [None] * spg
            for v in range(pitch):
                inv_cnt = 1.0 / jnp.minimum(win, lane_row + (pos0 + v + 1)).astype(F32)
                for ci in range(spg):
                    cur = rows_at(ci, v)
                    if v == 0:
                        tots[ci] = cur
                        for d in range(1, win):
                            tots[ci] = tots[ci] + rows_at(ci, -d)
                    else:
                        tots[ci] = tots[ci] + cur - rows_at(ci, v - win)
                    dbuf_ref[s, ci, pl.ds(v, SUBLANES, stride=pitch), :] = tots[ci] * inv_cnt - cur
        for ci in range(spg):
            c = gidx * spg + ci
            _roll_history(ext_ref, state_out_ref, functools.partial(_slab_rows,c),
                          functools.partial(_slab_lanes,c), hist, hpad, nseq, seg, chained=not has_state)
        dlt = jnp.concatenate(
            [jnp.concatenate([dbuf_ref[s, ci, 0:seg, :] for ci in range(spg)], axis=1) for s in range(nseq)],
            axis=0)
        yp = _dot(dlt.astype(BF16), wpool_ref[gidx]) * scale_ref[:, cols]
        y_ref[:, cols] = (yp * _silu(z)).astype(BF16)

    _finish(x_ref, mod_ref, _dot(y_ref[...], wout_ref[...]), gfin_ref, o_ref, nseq, seg)


def _const_spec(arr):
    nd = arr.ndim
    return pl.BlockSpec(arr.shape, lambda b, t, _nd=nd: (0,) * _nd, pipeline_mode=pl.Buffered(1))


def _run_layer(body, name, x, mod, g, state, weights, g_final, hist, hpad, nseq, seg, n_batch_steps,
               n_tiles, chained, extra_scratch, emit_v=False, with_state_out=True, ext_shape=None, stage_y=True):
    rows_t = nseq * seg
    m_rows = x.shape[0]
    seq_blk = 1 if chained else nseq
    n_seq_total = n_batch_steps * seq_blk
    in_arrays = [x, mod, g]
    in_specs = [
        pl.BlockSpec((rows_t, D_MODEL), lambda b, t: (b * n_tiles + t, 0)),
        pl.BlockSpec((seq_blk, 3, D_MODEL), lambda b, t: (b, 0, 0)),
        _const_spec(g),
    ]
    if state is not None:
        in_arrays.append(state)
        in_specs.append(_const_spec(state))
    for w in weights:
        in_arrays.append(w)
        in_specs.append(_const_spec(w))
    if g_final is not None:
        in_arrays.append(g_final)
        in_specs.append(_const_spec(g_final))

    out_shapes = [jax.ShapeDtypeStruct((m_rows, D_MODEL), F32)]
    out_specs = [pl.BlockSpec((rows_t, D_MODEL), lambda b, t: (b * n_tiles + t, 0))]
    if with_state_out:
        out_shapes.append(jax.ShapeDtypeStruct((n_seq_total, hist, D_INNER), F32))
        out_specs.append(pl.BlockSpec((seq_blk, hist, D_INNER), lambda b, t: (b, 0, 0)))
    if emit_v:
        out_shapes.append(jax.ShapeDtypeStruct((m_rows, D_INNER), F32))
        out_specs.append(pl.BlockSpec((rows_t, D_INNER), lambda b, t: (b * n_tiles + t, 0)))

    scratch = [pltpu.VMEM((rows_t, D_MODEL), BF16)]
    if with_state_out:
        scratch.append(pltpu.VMEM(ext_shape or (nseq, hpad + seg, D_INNER), F32))
    scratch += extra_scratch
    if stage_y:
        scratch.append(pltpu.VMEM((rows_t, D_INNER), BF16))

    return pl.pallas_call(
        body,
        grid=(n_batch_steps, n_tiles),
        in_specs=in_specs,
        out_specs=out_specs,
        out_shape=out_shapes,
        scratch_shapes=scratch,
        compiler_params=pltpu.CompilerParams(
            dimension_semantics=("arbitrary", "arbitrary"), vmem_limit_bytes=VMEM_LIMIT_BYTES),
        name=name,
    )(*in_arrays)


def _mix_matrices(w_s_a, b_s_a, seg):
    blk = MLP_BLOCK if seg >= MLP_BLOCK else seg
    cidx = jnp.arange(blk) // CHUNK
    mask = cidx[:, None] >= cidx[None, :]
    w = jnp.where(mask[None], w_s_a[:, :blk, :blk], jnp.zeros((), w_s_a.dtype))
    reps = MIX_ROWS // blk
    eye = jnp.eye(reps, dtype=w.dtype)
    wmix = jnp.einsum('rs,gij->grisj', eye, w).reshape(MLP_GROUPS, reps * blk, reps * blk)
    bmix = jnp.tile(b_s_a[:, :blk], (1, reps))[:, :, None]
    return wmix.astype(BF16), bmix.astype(F32)


def _trunk(x, mod, state_b, state_c, state_d, nseq, seg, n_batch_steps, n_tiles, emit_v, prm):
    has_state = state_b is not None
    rows_t = nseq * seg
    row = lambda v: v.reshape(1, -1)
    wmix, bmix = _mix_matrices(prm["w_s_a"], prm["b_s_a"], seg)
    common = dict(nseq=nseq, seg=seg, n_batch_steps=n_batch_steps, n_tiles=n_tiles, chained=not has_state)

    res = _run_layer(
        functools.partial(_layer_a_kernel, nseq, seg, emit_v, False), "layer_a",
        x, mod[0], row(prm["g_norm"][0]), None,
        [prm["wa_in"], row(prm["ln_a_g"]), row(prm["ln_a_b"]), wmix, bmix, prm["wa_out"]],
        None, 0, 0, extra_scratch=[pltpu.VMEM((rows_t, D_INNER), F32)], emit_v=emit_v,
        with_state_out=False, **common)
    x = res[0]
    v_rows = res[1] if emit_v else None

    hist, hpad = SCONV_W - 1, SUBLANES
    x, conv_b = _run_layer(
        functools.partial(_layer_b_kernel, nseq, seg, has_state, False), "layer_b",
        x, mod[1], row(prm["g_norm"][1]), state_b,
        [prm["wb_in"], prm["w_conv_b"], prm["wb_out"]],
        None, hist, hpad, extra_scratch=[], **common)

    hist, hpad = CCONV_W - 1, _round_up(CCONV_W - 1, SUBLANES)
    pitch = _group_pitch(seg)
    x, conv_c = _run_layer(
        functools.partial(_layer_c_kernel, nseq, seg, has_state, False), "layer_c",
        x, mod[2], row(prm["g_norm"][2]), state_c,
        [prm["wc_in"], prm["w_conv_c"], row(prm["b_conv_c"]), row(prm["ln_c_g"]),
         row(prm["ln_c_b"]), prm["wc_out"]],
        None, hist, hpad,
        extra_scratch=[pltpu.VMEM((nseq, D_INNER // LANES, SUBLANES * pitch, LANES), F32),
                       pltpu.VMEM((rows_t, D_INNER), F32),
                       pltpu.VMEM((3, rows_t, COL_CHUNK), F32)],
        ext_shape=(nseq, D_INNER // LANES, hpad + SUBLANES * pitch, LANES), stage_y=False, **common)

    hist, hpad = POOL_HIST, _round_up(POOL_HIST, SUBLANES)
    y, pool_d = _run_layer(
        functools.partial(_layer_d_kernel, nseq, seg, has_state, True), "layer_d",
        x, mod[3], row(prm["g_norm"][3]), state_d,
        [prm["wd_in"], prm["wd_pool"], row(prm["scale_pool_d"]), prm["wd_out"]],
        row(prm["g_final"]), hist, hpad,
        extra_scratch=[pltpu.VMEM((nseq, POOL_GROUP_W // LANES, SUBLANES * pitch, LANES), F32)],
        ext_shape=(nseq, D_INNER // LANES, hpad + SUBLANES * pitch, LANES), **common)
    return y, v_rows, conv_b, conv_c, pool_d


def kernel(x_prompt, x_sample, state_conv_b, state_conv_c, state_pool_d, c_prompt, c_sample, w_ada, b_ada, g_norm, w_in_a, ln_a_g, ln_a_b, w_s_a, b_s_a, w_out_a, w_in_b, w_conv_b, w_out_b, w_in_c, w_conv_c, b_conv_c, ln_c_g, ln_c_b, w_out_c, w_in_d, w_pool_d, scale_pool_d, w_out_d, g_final):
    n_p, s_p, _ = x_prompt.shape
    n_s, s_s, _ = x_sample.shape

    c_all = jnp.concatenate([c_prompt, c_sample], axis=0)
    pad = _round_up(n_p + n_s, SUBLANES) - (n_p + n_s)
    c_all = jnp.pad(c_all, ((0, pad), (0, 0)))
    mod = _modulation(c_all, w_ada, b_ada)
    mod_p = mod[:, :n_p].reshape(DEPTH, n_p, 3, D_MODEL)
    mod_s = mod[:, n_p:n_p + n_s].reshape(DEPTH, n_s, 3, D_MODEL)

    prm = dict(
        g_norm=g_norm, ln_a_g=ln_a_g, ln_a_b=ln_a_b, w_s_a=w_s_a, b_s_a=b_s_a,
        w_conv_b=w_conv_b, w_conv_c=w_conv_c, b_conv_c=b_conv_c, ln_c_g=ln_c_g, ln_c_b=ln_c_b,
        scale_pool_d=scale_pool_d, g_final=g_final,
        wa_in=w_in_a.astype(BF16),
        wa_out=w_out_a.astype(BF16),
        wb_in=w_in_b.astype(BF16),
        wb_out=w_out_b.astype(BF16),
        wc_in=w_in_c.astype(BF16),
        wc_out=w_out_c.astype(BF16),
        wd_in=w_in_d.astype(BF16),
        wd_pool=w_pool_d.astype(BF16),
        wd_out=w_out_d.astype(BF16),
    )

    y_p, _, cb_p, cc_p, pd_p = _trunk(
        x_prompt.reshape(n_p * s_p, D_MODEL), mod_p, None, None, None,
        nseq=ROW_TILE // SEG_ROWS, seg=SEG_ROWS, n_batch_steps=n_p, n_tiles=s_p // ROW_TILE, emit_v=False, prm=prm)
    y_s, v_s, cb_s, cc_s, pd_s = _trunk(
        x_sample.reshape(n_s * s_s, D_MODEL), mod_s, state_conv_b, state_conv_c, state_pool_d,
        nseq=n_s, seg=s_s, n_batch_steps=1, n_tiles=1, emit_v=True, prm=prm)

    return (y_p.reshape(n_p, s_p, D_MODEL), y_s.reshape(n_s, s_s, D_MODEL),
            v_s.reshape(n_s, s_s, D_INNER), cb_p, cb_s, cc_p, cc_s, pd_p, pd_s)
```

```python
import functools

import jax
import jax.numpy as jnp
from jax import lax
from jax.experimental import pallas as pl
from jax.experimental.pallas import tpu as pltpu

D_MODEL = 1024
D_INNER = 2048
DEPTH = 4
CHUNK = 64
MLP_BLOCK = 128
MLP_GROUPS = 8
MLP_GROUP_W = D_INNER // MLP_GROUPS
SCONV_W = 3
CCONV_W = 31
POOL_WINDOWS = (2, 4, 8, 16)
POOL_GROUP_W = D_INNER // len(POOL_WINDOWS)
POOL_HIST = max(POOL_WINDOWS) - 1
PAST_LEN = 2048
EPS = 1e-6

F32 = jnp.float32
BF16 = jnp.bfloat16

SUBLANES = 8
LANES = 128
MXU_COLS = 256
VACC_UNITS = 4
VMEM_LIMIT_BYTES = 58 * 1024 * 1024

ROW_TILE = 512
SEG_ROWS = 256
MIX_ROWS = 256
COL_CHUNK = 512
INV_SQRT2 = 0.7071067811865476


def _round_up(n, m):
    return (n + m - 1) // m * m


def _gelu(x):
    return 0.5 * x * (1.0 + lax.erf(x * INV_SQRT2))


def _silu(x):
    return x * jax.nn.sigmoid(x)


def _dot(a, b):
    return jnp.dot(a, b, preferred_element_type=F32)


def _mod_kernel(c_ref, w_ref, b_ref, o_ref):
    c = c_ref[...]
    a = _silu(c).astype(BF16)
    o_ref[0] = _dot(a, w_ref[0].astype(BF16)) + b_ref[0]


def _modulation(c_all, w_ada, b_ada):
    rows = c_all.shape[0]
    nb = 1024
    return pl.pallas_call(
        _mod_kernel,
        grid=(DEPTH, 3 * D_MODEL // nb),
        in_specs=[
            pl.BlockSpec((rows, D_MODEL), lambda i, j: (0, 0)),
            pl.BlockSpec((1, D_MODEL, nb), lambda i, j: (i, 0, j)),
            pl.BlockSpec((1, 1, nb), lambda i, j: (i, 0, j)),
        ],
        out_specs=pl.BlockSpec((1, rows, nb), lambda i, j: (i, 0, j)),
        out_shape=jax.ShapeDtypeStruct((DEPTH, rows, 3 * D_MODEL), F32),
        compiler_params=pltpu.CompilerParams(
            dimension_semantics=("arbitrary", "arbitrary"), vmem_limit_bytes=VMEM_LIMIT_BYTES),
        name="ada_modulation",
    )(c_all, w_ada, b_ada.reshape(DEPTH, 1, 3 * D_MODEL))


def _prenorm(x_ref, mod_ref, g_ref, h_ref, nseq, seg):
    g = g_ref[...]
    for s in range(nseq):
        rows = slice(s * seg, (s + 1) * seg)
        x = x_ref[rows, :]
        ms = jnp.mean(x * x, axis=-1, keepdims=True)
        y = (x * lax.rsqrt(ms + EPS)) * g
        m = s if mod_ref.shape[0] > 1 else 0
        h = y * (1.0 + mod_ref[m, 1:2, :]) + mod_ref[m, 0:1, :]
        h_ref[rows, :] = h.astype(BF16)


def _finish(x_ref, mod_ref, out, gfin_ref, o_ref, nseq, seg):
    for s in range(nseq):
        rows = slice(s * seg, (s + 1) * seg)
        m = s if mod_ref.shape[0] > 1 else 0
        xn = x_ref[rows, :] + mod_ref[m, 2:3, :] * out[rows, :]
        if gfin_ref is not None:
            ms = jnp.mean(xn * xn, axis=-1, keepdims=True)
            xn = (xn * lax.rsqrt(ms + EPS)) * gfin_ref[...]
        o_ref[rows, :] = xn


def _init_history(ext_ref, state_ref, hist, hpad, nseq):
    if state_ref is None:
        @pl.when(pl.program_id(1) == 0)
        def _():
            ext_ref[:, 0:hpad, :] = jnp.zeros((nseq, hpad, D_INNER), F32)
    else:
        for s in range(nseq):
            ext_ref[s, hpad - hist:hpad, :] = state_ref[s]


def _init_slab_history(ext_ref, state_ref, hist, hpad, nseq, seg):
    _, n_slab, ext_rows, _ = ext_ref.shape
    tail = ext_rows - hpad - seg
    if state_ref is None:
        @pl.when(pl.program_id(1) == 0)
        def _():
            ext_ref[:, :, 0:hpad, :] = jnp.zeros((nseq, n_slab, hpad, LANES), F32)
            if tail:
                ext_ref[:, :, hpad + seg:ext_rows, :] = jnp.zeros((nseq, n_slab, tail, LANES), F32)
    else:
        for s in range(nseq):
            for c in range(n_slab):
                ext_ref[s, c, hpad - hist:hpad, :] = state_ref[s, :, c * LANES:(c + 1) * LANES]
        if tail:
            ext_ref[:, :, hpad + seg:ext_rows, :] = jnp.zeros((nseq, n_slab, tail, LANES), F32)


def _slab_rows(c, s, r0, r1):
    return (s, c, slice(r0, r1), slice(None))


def _slab_lanes(c, s):
    return (s, slice(None), slice(c * LANES, (c + 1) * LANES))


def _group_pitch(seg):
    pitch = seg // SUBLANES
    while pitch % 8 == 0:
        pitch += 1
    return pitch


def _link_history(ext_ref, at, hist, hpad, nseq, seg):
    for s in range(1, nseq):
        ext_ref[at(s, hpad - hist, hpad)] = ext_ref[at(s - 1, hpad + seg - hist, hpad + seg)]


def _roll_history(ext_ref, state_out_ref, at, out_at, hist, hpad, nseq, seg, chained):
    if chained:
        last = ext_ref[at(nseq - 1, hpad + seg - hist, hpad + seg)]
        state_out_ref[out_at(0)] = last
        ext_ref[at(0, hpad - hist, hpad)] = last
    else:
        for s in range(nseq):
            last = ext_ref[at(s, hpad + seg - hist, hpad + seg)]
            state_out_ref[out_at(s)] = last
            ext_ref[at(s, hpad - hist, hpad)] = last


def _layer_a_kernel(nseq, seg, emit_v, has_final, *refs):
    refs = list(refs)
    x_ref, mod_ref, g_ref, win_ref, lng_ref, lnb_ref, wmix_ref, bmix_ref, wout_ref = refs[:9]
    refs = refs[9:]
    gfin_ref = refs.pop(0) if has_final else None
    o_ref = refs.pop(0)
    v_out_ref = refs.pop(0) if emit_v else None
    h_ref, vbuf_ref, y_ref = refs
    rows_t = nseq * seg
    mix_rows = wmix_ref.shape[1]
    nch = D_INNER // COL_CHUNK

    _prenorm(x_ref, mod_ref, g_ref, h_ref, nseq, seg)
    h = h_ref[...]

    shift = s1 = s2 = None
    for j in range(nch):
        cols = slice(j * COL_CHUNK, (j + 1) * COL_CHUNK)
        gv = _gelu(_dot(h, win_ref[:, D_INNER + j * COL_CHUNK:D_INNER + (j + 1) * COL_CHUNK]))
        vbuf_ref[:, cols] = gv
        if j == 0:
            shift = jnp.mean(gv, axis=-1, keepdims=True)
        dv = gv - shift
        a1 = jnp.sum(dv, axis=-1, keepdims=True)
        a2 = jnp.sum(dv * dv, axis=-1, keepdims=True)
        s1 = a1 if s1 is None else s1 + a1
        s2 = a2 if s2 is None else s2 + a2
    m1 = s1 * (1.0 / D_INNER)
    mu = shift + m1
    var = s2 * (1.0 / D_INNER) - m1 * m1
    rstd = lax.rsqrt(var + EPS)

    gpc = COL_CHUNK // MLP_GROUP_W
    for j in range(nch):
        cols = slice(j * COL_CHUNK, (j + 1) * COL_CHUNK)
        vn = (vbuf_ref[:, cols] - mu) * rstd * lng_ref[:, cols] + lnb_ref[:, cols]
        if emit_v:
            v_out_ref[:, cols] = vn
        vnb = vn.astype(BF16)
        mixed = []
        for gi in range(gpc):
            g = j * gpc + gi
            gcols = slice(gi * MLP_GROUP_W, (gi + 1) * MLP_GROUP_W)
            mixed.append(jnp.concatenate(
                [_dot(wmix_ref[g], vnb[r0:r0 + mix_rows, gcols]) + bmix_ref[g]
                 for r0 in range(0, rows_t, mix_rows)], axis=0))
        mixed = jnp.concatenate(mixed, axis=1)
        u = _gelu(_dot(h, win_ref[:, cols]))
        z = _dot(h, win_ref[:, 2 * D_INNER + j * COL_CHUNK:2 * D_INNER + (j + 1) * COL_CHUNK])
        y_ref[:, cols] = (u * mixed * _silu(z)).astype(BF16)

    _finish(x_ref, mod_ref, _dot(y_ref[...], wout_ref[...]), gfin_ref, o_ref, nseq, seg)


def _layer_b_kernel(nseq, seg, has_state, has_final, *refs):
    refs = list(refs)
    x_ref, mod_ref, g_ref = refs[:3]
    refs = refs[3:]
    state_ref = refs.pop(0) if has_state else None
    win_ref, wconv_ref, wout_ref = refs[:3]
    refs = refs[3:]
    gfin_ref = refs.pop(0) if has_final else None
    o_ref, state_out_ref, h_ref, ext_ref, y_ref = refs
    hist, hpad = SCONV_W - 1, SUBLANES
    nch = D_INNER // COL_CHUNK
    cc = COL_CHUNK

    _prenorm(x_ref, mod_ref, g_ref, h_ref, nseq, seg)
    _init_history(ext_ref, state_ref, hist, hpad, nseq)
    h = h_ref[...]

    for j in range(nch):
        cols = slice(j * cc, (j + 1) * cc)
        bg, cg, hv, z = [
            _dot(h, win_ref[:, br * D_INNER + j * cc:br * D_INNER + (j + 1) * cc]) for br in range(4)]
        p = cg * hv
        gate = bg * _silu(z)
        at = lambda s, r0, r1, cols=cols: (s, slice(r0, r1), cols)
        for s in range(nseq):
            ext_ref[s, hpad:hpad + seg, cols] = p[s * seg:(s + 1) * seg, :]
        if not has_state:
            _link_history(ext_ref, at, hist, hpad, nseq, seg)
        for s in range(nseq):
            rows = slice(s * seg, (s + 1) * seg)
            conv = p[rows, :] * wconv_ref[2:3, cols]
            for k in range(SCONV_W - 1):
                d = SCONV_W - 1 - k
                conv = conv + ext_ref[s, hpad - d:hpad - d + seg, cols] * wconv_ref[k:k + 1, cols]
            y_ref[rows, cols] = (conv * gate[rows, :]).astype(BF16)
        _roll_history(ext_ref, state_out_ref, at, lambda s, cols=cols: (s, slice(None), cols),
                      hist, hpad, nseq, seg, chained=not has_state)

    _finish(x_ref, mod_ref, _dot(y_ref[...], wout_ref[...]), gfin_ref, o_ref, nseq, seg)


def _layer_c_kernel(nseq, seg, has_state, has_final, *refs):
    refs = list(refs)
    x_ref, mod_ref, g_ref = refs[:3]
    refs = refs[3:]
    state_ref = refs.pop(0) if has_state else None
    win_ref, wconv_ref, bconv_ref, lng_ref, lnb_ref, wout_ref = refs[:6]
    refs = refs[6:]
    gfin_ref = refs.pop(0) if has_final else None
    o_ref, state_out_ref, h_ref, ext_ref, cbuf_ref, zs_ref, pbuf_ref, pk_ref = refs
    hist, hpad = CCONV_W - 1, _round_up(CCONV_W - 1, SUBLANES)
    nch = D_INNER // COL_CHUNK
    cc = COL_CHUNK
    pitch = _group_pitch(seg)
    n_src = pitch + CCONV_W - 1

    _init_slab_history(ext_ref, state_ref, hist, hpad, nseq, seg)
    _prenorm(x_ref, mod_ref, g_ref, h_ref, nseq, seg)

    h = h_ref[...]

    def proj_piece(branch, j, piece):
        lo = piece * MXU_COLS
        w_lo = branch * D_INNER + j * cc + lo
        pbuf_ref[branch, :, lo:lo + MXU_COLS] = _dot(h, win_ref[:, w_lo:w_lo + MXU_COLS])

    def conv_pair(s, p, buf):
        c0, c1 = 2 * p, 2 * p + 1
        for u in range(n_src):
            start = hpad - (CCONV_W - 1) + u
            pk_ref[buf, u] = jnp.concatenate(
                [ext_ref[s, c0, pl.ds(start, SUBLANES, stride=pitch), :],
                 ext_ref[s, c1, pl.ds(start, SUBLANES, stride=pitch), :]], axis=0).astype(BF16)
        bias = jnp.concatenate(
            [jnp.broadcast_to(bconv_ref[:, c * LANES:(c + 1) * LANES], (SUBLANES, LANES)) for c in (c0, c1)], axis=0)
        for v0 in range(0, pitch, VACC_UNITS):
            n_acc = min(VACC_UNITS, pitch - v0)
            accs = [None] * n_acc
            for k in range(CCONV_W):
                wk = wconv_ref[p, k].astype(F32)
                for i in range(n_acc):
                    term = pk_ref[buf, v0 + i + k].astype(F32) * wk
                    accs[i] = term if accs[i] is None else accs[i] + term
            for i in range(n_acc):
                res = accs[i] + bias
                cbuf_ref[s, c0, pl.ds(v0 + i, SUBLANES, stride=pitch), :] = res[0:SUBLANES]
                cbuf_ref[s, c1, pl.ds(v0 + i, SUBLANES, stride=pitch), :] = res[SUBLANES:2 * SUBLANES]

    def read_conv(j):
        return jnp.concatenate(
            [jnp.concatenate([cbuf_ref[s, j * (cc // LANES) + ci, 0:seg, :] for ci in range(cc // LANES)], axis=1)
             for s in range(nseq)], axis=0)

    shift = s1 = s2 = None
    n_piece = cc // MXU_COLS
    for piece in range(n_piece):
        proj_piece(0, 0, piece)
        proj_piece(1, 0, piece)
    for j in range(nch):
        glu = pbuf_ref[0] * jax.nn.sigmoid(pbuf_ref[1])
        for s in range(nseq):
            for ci in range(cc // LANES):
                ext_ref[s, j * (cc // LANES) + ci, hpad:hpad + seg, :] = (
                    glu[s * seg:(s + 1) * seg, ci * LANES:(ci + 1) * LANES])
        if not has_state:
            for ci in range(cc // LANES):
                _link_history(ext_ref, functools.partial(_slab_rows, j * (cc // LANES) + ci), hist, hpad, nseq, seg)
        pieces = [(2, j, p) for p in range(n_piece)]
        if j + 1 < nch:
            pieces += [(br, j + 1, p) for br in (0, 1) for p in range(n_piece)]
        units = [(s, j * (cc // LANES) // 2 + pi) for s in range(nseq) for pi in range(cc // LANES // 2)]
        for ui, (s, p) in enumerate(units):
            for _ in range(-(-len(pieces) // (len(units) - ui))):
                proj_piece(*pieces.pop(0))
            conv_pair(s, p, ui % 2)
        for ci in range(cc // LANES):
            c = j * (cc // LANES) + ci
            _roll_history(ext_ref, state_out_ref, functools.partial(_slab_rows, c),
                          functools.partial(_slab_lanes, c), hist, hpad, nseq, seg, chained=not has_state)
        zs_ref[:, j * cc:(j + 1) * cc] = _silu(pbuf_ref[2])
        cv = read_conv(j)
        if j == 0:
            shift = jnp.mean(cv, axis=-1, keepdims=True)
        dv = cv - shift
        a1 = jnp.sum(dv, axis=-1, keepdims=True)
        a2 = jnp.sum(dv * dv, axis=-1, keepdims=True)
        s1 = a1 if s1 is None else s1 + a1
        s2 = a2 if s2 is None else s2 + a2
    m1 = s1 * (1.0 / D_INNER)
    mu = shift + m1
    var = s2 * (1.0 / D_INNER) - m1 * m1
    rstd = lax.rsqrt(var + EPS)

    out = None
    for j in range(nch):
        cols = slice(j * cc, (j + 1) * cc)
        yn = (read_conv(j) - mu) * rstd * lng_ref[:, cols] + lnb_ref[:, cols]
        part = _dot((_silu(yn) * zs_ref[:, cols]).astype(BF16), wout_ref[cols, :])
        out = part if out is None else out + part

    _finish(x_ref, mod_ref, out, gfin_ref, o_ref, nseq, seg)


def _layer_d_kernel(nseq, seg, has_state, has_final, *refs):
    refs = list(refs)
    x_ref, mod_ref, g_ref = refs[:3]
    refs = refs[3:]
    state_ref = refs.pop(0) if has_state else None
    win_ref, wpool_ref, scale_ref, wout_ref = refs[:4]
    refs = refs[4:]
    gfin_ref = refs.pop(0) if has_final else None
    o_ref, state_out_ref, h_ref, ext_ref, dbuf_ref, y_ref = refs
    hist, hpad = POOL_HIST, _round_up(POOL_HIST, SUBLANES)
    cc = POOL_GROUP_W
    pitch = _group_pitch(seg)
    spg = cc // LANES

    _init_slab_history(ext_ref, state_ref, hist, hpad, nseq, seg)
    _prenorm(x_ref, mod_ref, g_ref, h_ref, nseq, seg)
    h = h_ref[...]

    lane_row = lax.broadcasted_iota(jnp.int32, (SUBLANES, LANES), 0) * pitch

    for gidx, win in enumerate(POOL_WINDOWS):
        cols = slice(gidx * cc, (gidx + 1) * cc)
        pv = _dot(h, win_ref[:, cols])
        z = _dot(h, win_ref[:, D_INNER + gidx * cc:D_INNER + (gidx + 1) * cc])
        for s in range(nseq):
            for ci in range(spg):
                ext_ref[s, gidx * spg + ci, hpad:hpad + seg, :] = (
                    pv[s * seg:(s + 1) * seg, ci * LANES:(ci + 1) * LANES])
        if not has_state:
            for ci in range(spg):
                _link_history(ext_ref, functools.partial(_slab_rows, gidx * spg + ci), hist, hpad, nseq, seg)
        for s in range(nseq):
            pos0 = PAST_LEN if has_state else pl.program_id(1) * (nseq * seg) + s * seg

            def rows_at(ci, r, s=s):
                return ext_ref[s, gidx * spg + ci, pl.ds(hpad + r, SUBLANES, stride=pitch), :]

            tots = [None] * spg
            for v in range(pitch):
                inv_cnt = 1.0 / jnp.minimum(win, lane_row + (pos0 + v + 1)).astype(F32)
                for ci in range(spg):
                    cur = rows_at(ci, v)
                    if v == 0:
                        tots[ci] = cur
                        for d in range(1, win):
                            tots[ci] = tots[ci] + rows_at(ci, -d)
                    else:
                        tots[ci] = tots[ci] + cur - rows_at(ci, v - win)
                    dbuf_ref[s, ci, pl.ds(v, SUBLANES, stride=pitch), :] = tots[ci] * inv_cnt - cur
        for ci in range(spg):
            c = gidx * spg + ci
            _roll_history(ext_ref, state_out_ref, functools.partial(_slab_rows, c),
                          functools.partial(_slab_lanes, c), hist, hpad, nseq, seg, chained=not has_state)
        dlt = jnp.concatenate(
            [jnp.concatenate([dbuf_ref[s, ci, 0:seg, :] for ci in range(spg)], axis=1) for s in range(nseq)],
            axis=0)
        yp = _dot(dlt.astype(BF16), wpool_ref[gidx]) * scale_ref[:, cols]
        y_ref[:, cols] = (yp * _silu(z)).astype(BF16)

    _finish(x_ref, mod_ref, _dot(y_ref[...], wout_ref[...]), gfin_ref, o_ref, nseq, seg)


def _const_spec(arr):
    nd = arr.ndim
    return pl.BlockSpec(arr.shape, lambda b, t, _nd=nd: (0,) * _nd, pipeline_mode=pl.Buffered(1))


def _run_layer(body, name, x, mod, g, state, weights, g_final, hist, hpad, nseq, seg, n_batch_steps,
               n_tiles, chained, extra_scratch, emit_v=False, with_state_out=True, ext_shape=None, stage_y=True):
    rows_t = nseq * seg
    m_rows = x.shape[0]
    seq_blk = 1 if chained else nseq
    n_seq_total = n_batch_steps * seq_blk
    in_arrays = [x, mod, g]
    in_specs = [
        pl.BlockSpec((rows_t, D_MODEL), lambda b, t: (b * n_tiles + t, 0)),
        pl.BlockSpec((seq_blk, 3, D_MODEL), lambda b, t: (b, 0, 0)),
        _const_spec(g),
    ]
    if state is not None:
        in_arrays.append(state)
        in_specs.append(_const_spec(state))
    for w in weights:
        in_arrays.append(w)
        in_specs.append(_const_spec(w))
    if g_final is not None:
        in_arrays.append(g_final)
        in_specs.append(_const_spec(g_final))

    out_shapes = [jax.ShapeDtypeStruct((m_rows, D_MODEL), F32)]
    out_specs = [pl.BlockSpec((rows_t, D_MODEL), lambda b, t: (b * n_tiles + t, 0))]
    if with_state_out:
        out_shapes.append(jax.ShapeDtypeStruct((n_seq_total, hist, D_INNER), F32))
        out_specs.append(pl.BlockSpec((seq_blk, hist, D_INNER), lambda b, t: (b, 0, 0)))
    if emit_v:
        out_shapes.append(jax.ShapeDtypeStruct((m_rows, D_INNER), F32))
        out_specs.append(pl.BlockSpec((rows_t, D_INNER), lambda b, t: (b * n_tiles + t, 0)))

    scratch = [pltpu.VMEM((rows_t, D_MODEL), BF16)]
    if with_state_out:
        scratch.append(pltpu.VMEM(ext_shape or (nseq, hpad + seg, D_INNER), F32))
    scratch += extra_scratch
    if stage_y:
        scratch.append(pltpu.VMEM((rows_t, D_INNER), BF16))

    return pl.pallas_call(
        body,
        grid=(n_batch_steps, n_tiles),
        in_specs=in_specs,
        out_specs=out_specs,
        out_shape=out_shapes,
        scratch_shapes=scratch,
        compiler_params=pltpu.CompilerParams(
            dimension_semantics=("arbitrary", "arbitrary"), vmem_limit_bytes=VMEM_LIMIT_BYTES),
        name=name,
    )(*in_arrays)


def _pack_conv_weights(w):
    taps = w.shape[0]
    pairs = D_INNER // (2 * LANES)
    w = w.reshape(taps, pairs, 2, 1, LANES).transpose(1, 0, 2, 3, 4)
    w = jnp.broadcast_to(w, (pairs, taps, 2, SUBLANES, LANES))
    return w.reshape(pairs, taps, 2 * SUBLANES, LANES).astype(BF16)


def _mix_matrices(w_s_a, b_s_a, seg):
    blk = MLP_BLOCK if seg >= MLP_BLOCK else seg
    cidx = jnp.arange(blk) // CHUNK
    mask = cidx[:, None] >= cidx[None, :]
    w = jnp.where(mask[None], w_s_a[:, :blk, :blk], jnp.zeros((), w_s_a.dtype))
    reps = MIX_ROWS // blk
    eye = jnp.eye(reps, dtype=w.dtype)
    wmix = jnp.einsum('rs,gij->grisj', eye, w).reshape(MLP_GROUPS, reps * blk, reps * blk)
    bmix = jnp.tile(b_s_a[:, :blk], (1, reps))[:, :, None]
    return wmix.astype(BF16), bmix.astype(F32)


def _trunk(x, mod, state_b, state_c, state_d, nseq, seg, n_batch_steps, n_tiles, emit_v, prm):
    has_state = state_b is not None
    rows_t = nseq * seg
    row = lambda v: v.reshape(1, -1)
    wmix, bmix = _mix_matrices(prm["w_s_a"], prm["b_s_a"], seg)
    common = dict(nseq=nseq, seg=seg, n_batch_steps=n_batch_steps, n_tiles=n_tiles, chained=not has_state)

    res = _run_layer(
        functools.partial(_layer_a_kernel, nseq, seg, emit_v, False), "layer_a",
        x, mod[0], row(prm["g_norm"][0]), None,
        [prm["wa_in"], row(prm["ln_a_g"]), row(prm["ln_a_b"]), wmix, bmix, prm["wa_out"]],
        None, 0, 0, extra_scratch=[pltpu.VMEM((rows_t, D_INNER), F32)], emit_v=emit_v,
        with_state_out=False, **common)
    x = res[0]
    v_rows = res[1] if emit_v else None

    hist, hpad = SCONV_W - 1, SUBLANES
    x, conv_b = _run_layer(
        functools.partial(_layer_b_kernel, nseq, seg, has_state, False), "layer_b",
        x, mod[1], row(prm["g_norm"][1]), state_b,
        [prm["wb_in"], prm["w_conv_b"], prm["wb_out"]],
        None, hist, hpad, extra_scratch=[], **common)

    hist, hpad = CCONV_W - 1, _round_up(CCONV_W - 1, SUBLANES)
    pitch = _group_pitch(seg)
    x, conv_c = _run_layer(
        functools.partial(_layer_c_kernel, nseq, seg, has_state, False), "layer_c",
        x, mod[2], row(prm["g_norm"][2]), state_c,
        [prm["wc_in"], prm["wc_conv"], row(prm["b_conv_c"]), row(prm["ln_c_g"]),
         row(prm["ln_c_b"]), prm["wc_out"]],
        None, hist, hpad,
        extra_scratch=[pltpu.VMEM((nseq, D_INNER // LANES, SUBLANES * pitch, LANES), F32),
                       pltpu.VMEM((rows_t, D_INNER), F32),
                       pltpu.VMEM((3, rows_t, COL_CHUNK), F32),
                       pltpu.VMEM((2, pitch + CCONV_W - 1, 2 * SUBLANES, LANES), BF16)],
        ext_shape=(nseq, D_INNER // LANES, hpad + SUBLANES * pitch, LANES), stage_y=False, **common)

    hist, hpad = POOL_HIST, _round_up(POOL_HIST, SUBLANES)
    y, pool_d = _run_layer(
        functools.partial(_layer_d_kernel, nseq, seg, has_state, True), "layer_d",
        x, mod[3], row(prm["g_norm"][3]), state_d,
        [prm["wd_in"], prm["wd_pool"], row(prm["scale_pool_d"]), prm["wd_out"]],
        row(prm["g_final"]), hist, hpad,
        extra_scratch=[pltpu.VMEM((nseq, POOL_GROUP_W // LANES, SUBLANES * pitch, LANES), F32)],
        ext_shape=(nseq, D_INNER // LANES, hpad + SUBLANES * pitch, LANES), **common)
    return y, v_rows, conv_b, conv_c, pool_d


def kernel(x_prompt, x_sample, state_conv_b, state_conv_c, state_pool_d, c_prompt, c_sample, w_ada, b_ada, g_norm, w_in_a, ln_a_g, ln_a_b, w_s_a, b_s_a, w_out_a, w_in_b, w_conv_b, w_out_b, w_in_c, w_conv_c, b_conv_c, ln_c_g, ln_c_b, w_out_c, w_in_d, w_pool_d, scale_pool_d, w_out_d, g_final):
    n_p, s_p, _ = x_prompt.shape
    n_s, s_s, _ = x_sample.shape

    c_all = jnp.concatenate([c_prompt, c_sample], axis=0)
    pad = _round_up(n_p + n_s, SUBLANES) - (n_p + n_s)
    c_all = jnp.pad(c_all, ((0, pad), (0, 0)))
    mod = _modulation(c_all, w_ada, b_ada)
    mod_p = mod[:, :n_p].reshape(DEPTH, n_p, 3, D_MODEL)
    mod_s = mod[:, n_p:n_p + n_s].reshape(DEPTH, n_s, 3, D_MODEL)

    prm = dict(
        g_norm=g_norm, ln_a_g=ln_a_g, ln_a_b=ln_a_b, w_s_a=w_s_a, b_s_a=b_s_a,
        w_conv_b=w_conv_b, b_conv_c=b_conv_c, ln_c_g=ln_c_g, ln_c_b=ln_c_b,
        scale_pool_d=scale_pool_d, g_final=g_final,
        wa_in=w_in_a.astype(BF16),
        wa_out=w_out_a.astype(BF16),
        wb_in=w_in_b.astype(BF16),
        wb_out=w_out_b.astype(BF16),
        wc_in=w_in_c.astype(BF16),
        wc_conv=_pack_conv_weights(w_conv_c),
        wc_out=w_out_c.astype(BF16),
        wd_in=w_in_d.astype(BF16),
        wd_pool=w_pool_d.astype(BF16),
        wd_out=w_out_d.astype(BF16),
    )

    y_p, _, cb_p, cc_p, pd_p = _trunk(
        x_prompt.reshape(n_p * s_p, D_MODEL), mod_p, None, None, None,
        nseq=ROW_TILE // SEG_ROWS, seg=SEG_ROWS, n_batch_steps=n_p, n_tiles=s_p // ROW_TILE, emit_v=False, prm=prm)
    y_s, v_s, cb_s, cc_s, pd_s = _trunk(
        x_sample.reshape(n_s * s_s, D_MODEL), mod_s, state_conv_b, state_conv_c, state_pool_d,
        nseq=n_s, seg=s_s, n_batch_steps=1, n_tiles=1, emit_v=True, prm=prm)

    return (y_p.reshape(n_p, s_p, D_MODEL), y_s.reshape(n_s, s_s, D_MODEL),
            v_s.reshape(n_s, s_s, D_INNER), cb_p, cb_s, cc_p, cc_s, pd_p, pd_s)
```

```python
import functools

import jax
import jax.numpy as jnp
from jax import lax
from jax.experimental import pallas as pl
from jax.experimental.pallas import tpu as pltpu

D_MODEL = 1024
D_INNER = 2048
DEPTH = 4
CHUNK = 64
MLP_BLOCK = 128
MLP_GROUPS = 8
MLP_GROUP_W = D_INNER // MLP_GROUPS
SCONV_W = 3
CCONV_W = 31
POOL_WINDOWS = (2, 4, 8, 16)
POOL_GROUP_W = D_INNER // len(POOL_WINDOWS)
POOL_HIST = max(POOL_WINDOWS) - 1
PAST_LEN = 2048
EPS = 1e-6

F32 = jnp.float32
BF16 = jnp.bfloat16

SUBLANES = 8
LANES = 128
MXU_COLS = 256
VACC_UNITS = 4
CONV_BLOCK = 8
VMEM_LIMIT_BYTES = 58 * 1024 * 1024

ROW_TILE = 512
SEG_ROWS = 256
MIX_ROWS = 256
COL_CHUNK = 512
INV_SQRT2 = 0.7071067811865476


def _round_up(n, m):
    return (n + m - 1) // m * m


def _gelu(x):
    return 0.5 * x * (1.0 + lax.erf(x * INV_SQRT2))


def _silu(x):
    return x * jax.nn.sigmoid(x)


def _dot(a, b):
    return jnp.dot(a, b, preferred_element_type=F32)


def _mod_kernel(c_ref, w_ref, b_ref, o_ref):
    c = c_ref[...]
    a = _silu(c).astype(BF16)
    o_ref[0] = _dot(a, w_ref[0].astype(BF16)) + b_ref[0]


def _modulation(c_all, w_ada, b_ada):
    rows = c_all.shape[0]
    nb = 1024
    return pl.pallas_call(
        _mod_kernel,
        grid=(DEPTH, 3 * D_MODEL // nb),
        in_specs=[
            pl.BlockSpec((rows, D_MODEL), lambda i, j: (0, 0)),
            pl.BlockSpec((1, D_MODEL, nb), lambda i, j: (i, 0, j)),
            pl.BlockSpec((1, 1, nb), lambda i, j: (i, 0, j)),
        ],
        out_specs=pl.BlockSpec((1, rows, nb), lambda i, j: (i, 0, j)),
        out_shape=jax.ShapeDtypeStruct((DEPTH, rows, 3 * D_MODEL), F32),
        compiler_params=pltpu.CompilerParams(
            dimension_semantics=("arbitrary", "arbitrary"), vmem_limit_bytes=VMEM_LIMIT_BYTES),
        name="ada_modulation",
    )(c_all, w_ada, b_ada.reshape(DEPTH, 1, 3 * D_MODEL))


def _prenorm(x_ref, mod_ref, g_ref, h_ref, nseq, seg):
    g = g_ref[...]
    for s in range(nseq):
        rows = slice(s * seg, (s + 1) * seg)
        x = x_ref[rows, :]
        ms = jnp.mean(x * x, axis=-1, keepdims=True)
        y = (x * lax.rsqrt(ms + EPS)) * g
        m = s if mod_ref.shape[0] > 1 else 0
        h = y * (1.0 + mod_ref[m, 1:2, :]) + mod_ref[m, 0:1, :]
        h_ref[rows, :] = h.astype(BF16)


def _zero_of(value):
    word = lax.bitcast_convert_type(value, jnp.uint32)
    return lax.shift_right_logical(lax.shift_right_logical(word, jnp.uint32(16)), jnp.uint32(16))


def _finish(x_ref, mod_ref, out, gfin_ref, o_ref, nseq, seg):
    for s in range(nseq):
        rows = slice(s * seg, (s + 1) * seg)
        m = s if mod_ref.shape[0] > 1 else 0
        xn = x_ref[rows, :] + mod_ref[m, 2:3, :] * out[rows, :]
        if gfin_ref is not None:
            ms = jnp.mean(xn * xn, axis=-1, keepdims=True)
            xn = (xn * lax.rsqrt(ms + EPS)) * gfin_ref[...]
        o_ref[rows, :] = xn


def _init_history(ext_ref, state_ref, hist, hpad, nseq):
    if state_ref is None:
        @pl.when(pl.program_id(1) == 0)
        def _():
            ext_ref[:, 0:hpad, :] = jnp.zeros((nseq, hpad, D_INNER), F32)
    else:
        for s in range(nseq):
            ext_ref[s, hpad - hist:hpad, :] = state_ref[s]


def _init_slab_history(ext_ref, state_ref, hist, hpad, nseq, seg):
    _, n_slab, ext_rows, _ = ext_ref.shape
    tail = ext_rows - hpad - seg
    if state_ref is None:
        @pl.when(pl.program_id(1) == 0)
        def _():
            ext_ref[:, :, 0:hpad, :] = jnp.zeros((nseq, n_slab, hpad, LANES), F32)
            if tail:
                ext_ref[:, :, hpad + seg:ext_rows, :] = jnp.zeros((nseq, n_slab, tail, LANES), F32)
    else:
        for s in range(nseq):
            for c in range(n_slab):
                ext_ref[s, c, hpad - hist:hpad, :] = state_ref[s, :, c * LANES:(c + 1) * LANES]
        if tail:
            ext_ref[:, :, hpad + seg:ext_rows, :] = jnp.zeros((nseq, n_slab, tail, LANES), F32)


def _slab_rows(c, s, r0, r1):
    return (s, c, slice(r0, r1), slice(None))


def _slab_lanes(c, s):
    return (s, slice(None), slice(c * LANES, (c + 1) * LANES))


def _group_pitch(seg):
    pitch = seg // SUBLANES
    while pitch % 8 == 0:
        pitch += 1
    return pitch


def _link_history(ext_ref, at, hist, hpad, nseq, seg):
    for s in range(1, nseq):
        ext_ref[at(s, hpad - hist, hpad)] = ext_ref[at(s - 1, hpad + seg - hist, hpad + seg)]


def _roll_history(ext_ref, state_out_ref, at, out_at, hist, hpad, nseq, seg, chained):
    if chained:
        last = ext_ref[at(nseq - 1, hpad + seg - hist, hpad + seg)]
        state_out_ref[out_at(0)] = last
        ext_ref[at(0, hpad - hist, hpad)] = last
    else:
        for s in range(nseq):
            last = ext_ref[at(s, hpad + seg - hist, hpad + seg)]
            state_out_ref[out_at(s)] = last
            ext_ref[at(s, hpad - hist, hpad)] = last


def _layer_a_kernel(nseq, seg, emit_v, has_final, *refs):
    refs = list(refs)
    x_ref, mod_ref, g_ref, win_ref, lng_ref, lnb_ref, wmix_ref, bmix_ref, wout_ref = refs[:9]
    refs = refs[9:]
    gfin_ref = refs.pop(0) if has_final else None
    o_ref = refs.pop(0)
    v_out_ref = refs.pop(0) if emit_v else None
    h_ref, vbuf_ref, y_ref = refs
    rows_t = nseq * seg
    mix_rows = wmix_ref.shape[1]
    nch = D_INNER // COL_CHUNK

    _prenorm(x_ref, mod_ref, g_ref, h_ref, nseq, seg)
    h = h_ref[...]

    shift = s1 = s2 = None
    for j in range(nch):
        cols = slice(j * COL_CHUNK, (j + 1) * COL_CHUNK)
        gv = _gelu(_dot(h, win_ref[:, D_INNER + j * COL_CHUNK:D_INNER + (j + 1) * COL_CHUNK]))
        vbuf_ref[:, cols] = gv
        if j == 0:
            shift = jnp.mean(gv, axis=-1, keepdims=True)
        dv = gv - shift
        a1 = jnp.sum(dv, axis=-1, keepdims=True)
        a2 = jnp.sum(dv * dv, axis=-1, keepdims=True)
        s1 = a1 if s1 is None else s1 + a1
        s2 = a2 if s2 is None else s2 + a2
    m1 = s1 * (1.0 / D_INNER)
    mu = shift + m1
    var = s2 * (1.0 / D_INNER) - m1 * m1
    rstd = lax.rsqrt(var + EPS)

    gpc = COL_CHUNK // MLP_GROUP_W
    for j in range(nch):
        cols = slice(j * COL_CHUNK, (j + 1) * COL_CHUNK)
        vn = (vbuf_ref[:, cols] - mu) * rstd * lng_ref[:, cols] + lnb_ref[:, cols]
        if emit_v:
            v_out_ref[:, cols] = vn
        vnb = vn.astype(BF16)
        mixed = []
        for gi in range(gpc):
            g = j * gpc + gi
            gcols = slice(gi * MLP_GROUP_W, (gi + 1) * MLP_GROUP_W)
            mixed.append(jnp.concatenate(
                [_dot(wmix_ref[g], vnb[r0:r0 + mix_rows, gcols]) + bmix_ref[g]
                 for r0 in range(0, rows_t, mix_rows)], axis=0))
        mixed = jnp.concatenate(mixed, axis=1)
        u = _gelu(_dot(h, win_ref[:, cols]))
        z = _dot(h, win_ref[:, 2 * D_INNER + j * COL_CHUNK:2 * D_INNER + (j + 1) * COL_CHUNK])
        y_ref[:, cols] = (u * mixed * _silu(z)).astype(BF16)

    _finish(x_ref, mod_ref, _dot(y_ref[...], wout_ref[...]), gfin_ref, o_ref, nseq, seg)


def _layer_b_kernel(nseq, seg, has_state, has_final, *refs):
    refs = list(refs)
    x_ref, mod_ref, g_ref = refs[:3]
    refs = refs[3:]
    state_ref = refs.pop(0) if has_state else None
    win_ref, wconv_ref, wout_ref = refs[:3]
    refs = refs[3:]
    gfin_ref = refs.pop(0) if has_final else None
    o_ref, state_out_ref, h_ref, ext_ref, y_ref = refs
    hist, hpad = SCONV_W - 1, SUBLANES
    nch = D_INNER // COL_CHUNK
    cc = COL_CHUNK

    _prenorm(x_ref, mod_ref, g_ref, h_ref, nseq, seg)
    _init_history(ext_ref, state_ref, hist, hpad, nseq)
    h = h_ref[...]

    for j in range(nch):
        cols = slice(j * cc, (j + 1) * cc)
        bg, cg, hv, z = [
            _dot(h, win_ref[:, br * D_INNER + j * cc:br * D_INNER + (j + 1) * cc]) for br in range(4)]
        p = cg * hv
        gate = bg * _silu(z)
        at = lambda s, r0, r1, cols=cols: (s, slice(r0, r1), cols)
        for s in range(nseq):
            ext_ref[s, hpad:hpad + seg, cols] = p[s * seg:(s + 1) * seg, :]
        if not has_state:
            _link_history(ext_ref, at, hist, hpad, nseq, seg)
        for s in range(nseq):
            rows = slice(s * seg, (s + 1) * seg)
            conv = p[rows, :] * wconv_ref[2:3, cols]
            for k in range(SCONV_W - 1):
                d = SCONV_W - 1 - k
                conv = conv + ext_ref[s, hpad - d:hpad - d + seg, cols] * wconv_ref[k:k + 1, cols]
            y_ref[rows, cols] = (conv * gate[rows, :]).astype(BF16)
        _roll_history(ext_ref, state_out_ref, at, lambda s, cols=cols: (s, slice(None), cols),
                      hist, hpad, nseq, seg, chained=not has_state)

    _finish(x_ref, mod_ref, _dot(y_ref[...], wout_ref[...]), gfin_ref, o_ref, nseq, seg)


def _layer_c_kernel(nseq, seg, has_state, has_final, *refs):
    refs = list(refs)
    x_ref, mod_ref, g_ref = refs[:3]
    refs = refs[3:]
    state_ref = refs.pop(0) if has_state else None
    win_ref, wconv_ref, bconv_ref, lng_ref, lnb_ref, wout_ref = refs[:6]
    refs = refs[6:]
    gfin_ref = refs.pop(0) if has_final else None
    o_ref, state_out_ref, h_ref, ext_ref, cbuf_ref, zs_ref, pbuf_ref = refs
    hist, hpad = CCONV_W - 1, _round_up(CCONV_W - 1, SUBLANES)
    nch = D_INNER // COL_CHUNK
    cc = COL_CHUNK
    pitch = _group_pitch(seg)
    pairs_per_chunk = cc // LANES // 2

    _init_slab_history(ext_ref, state_ref, hist, hpad, nseq, seg)
    _prenorm(x_ref, mod_ref, g_ref, h_ref, nseq, seg)

    h = h_ref[...]

    def proj_piece(branch, j, piece):
        lo = piece * MXU_COLS
        w_lo = branch * D_INNER + j * cc + lo
        pbuf_ref[branch, :, lo:lo + MXU_COLS] = _dot(h, win_ref[:, w_lo:w_lo + MXU_COLS])

    def conv_unit(j, s, unit, carry):
        p = j * pairs_per_chunk + unit
        c0 = 2 * p
        bias = bconv_ref[p]
        prev = None
        srcs = {}
        for v0 in range(0, pitch, VACC_UNITS):
            n_acc = min(VACC_UNITS, pitch - v0)
            if v0 % CONV_BLOCK == 0:
                srcs = {}
                for u in range(v0, min(v0 + CONV_BLOCK, pitch) + CCONV_W - 1):
                    start = hpad - (CCONV_W - 1) + u
                    srcs[u] = jnp.concatenate(
                        [ext_ref[s, c0, pl.ds(start, SUBLANES, stride=pitch), :],
                         ext_ref[s, c0 + 1, pl.ds(start, SUBLANES, stride=pitch), :]], axis=0).astype(BF16)
            accs = [None] * n_acc
            for k in range(CCONV_W):
                wk = wconv_ref[p, k]
                if k == 0 and prev is not None:
                    wk = pltpu.bitcast(pltpu.bitcast(wk, jnp.uint32) | _zero_of(prev), BF16)
                wk = wk.astype(F32)
                for i in range(n_acc):
                    term = srcs[v0 + i + k].astype(F32) * wk
                    accs[i] = term if accs[i] is None else accs[i] + term
            prev = accs[n_acc - 1][0:SUBLANES]
            for i in range(n_acc):
                res = accs[i] + bias
                cbuf_ref[s, c0, pl.ds(v0 + i, SUBLANES, stride=pitch), :] = res[0:SUBLANES]
                cbuf_ref[s, c0 + 1, pl.ds(v0 + i, SUBLANES, stride=pitch), :] = res[SUBLANES:2 * SUBLANES]
        return carry

    def read_conv(j):
        return jnp.concatenate(
            [jnp.concatenate([cbuf_ref[s, j * (cc // LANES) + ci, 0:seg, :] for ci in range(cc // LANES)], axis=1)
             for s in range(nseq)], axis=0)

    shift = s1 = s2 = None
    n_piece = cc // MXU_COLS
    for piece in range(n_piece):
        proj_piece(0, 0, piece)
        proj_piece(1, 0, piece)
    for j in range(nch):
        glu = pbuf_ref[0] * jax.nn.sigmoid(pbuf_ref[1])
        for s in range(nseq):
            for ci in range(cc // LANES):
                ext_ref[s, j * (cc // LANES) + ci, hpad:hpad + seg, :] = (
                    glu[s * seg:(s + 1) * seg, ci * LANES:(ci + 1) * LANES])
        if not has_state:
            for ci in range(cc // LANES):
                _link_history(ext_ref, functools.partial(_slab_rows, j * (cc // LANES) + ci), hist, hpad, nseq, seg)
        for piece in range(n_piece):
            proj_piece(2, j, piece)
        for s in range(nseq):
            lax.fori_loop(0, pairs_per_chunk, functools.partial(conv_unit, j, s), 0)
        for ci in range(cc // LANES):
            c = j * (cc // LANES) + ci
            _roll_history(ext_ref, state_out_ref, functools.partial(_slab_rows, c),
                          functools.partial(_slab_lanes, c), hist, hpad, nseq, seg, chained=not has_state)
        zs_ref[:, j * cc:(j + 1) * cc] = _silu(pbuf_ref[2])
        if j + 1 < nch:
            for piece in range(n_piece):
                proj_piece(0, j + 1, piece)
                proj_piece(1, j + 1, piece)
        cv = read_conv(j)
        if j == 0:
            shift = jnp.mean(cv, axis=-1, keepdims=True)
        dv = cv - shift
        a1 = jnp.sum(dv, axis=-1, keepdims=True)
        a2 = jnp.sum(dv * dv, axis=-1, keepdims=True)
        s1 = a1 if s1 is None else s1 + a1
        s2 = a2 if s2 is None else s2 + a2
    m1 = s1 * (1.0 / D_INNER)
    mu = shift + m1
    var = s2 * (1.0 / D_INNER) - m1 * m1
    rstd = lax.rsqrt(var + EPS)

    out = None
    for j in range(nch):
        cols = slice(j * cc, (j + 1) * cc)
        yn = (read_conv(j) - mu) * rstd * lng_ref[:, cols] + lnb_ref[:, cols]
        part = _dot((_silu(yn) * zs_ref[:, cols]).astype(BF16), wout_ref[cols, :])
        out = part if out is None else out + part

    _finish(x_ref, mod_ref, out, gfin_ref, o_ref, nseq, seg)


def _layer_d_kernel(nseq, seg, has_state, has_final, *refs):
    refs = list(refs)
    x_ref, mod_ref, g_ref = refs[:3]
    refs = refs[3:]
    state_ref = refs.pop(0) if has_state else None
    win_ref, wpool_ref, scale_ref, wout_ref = refs[:4]
    refs = refs[4:]
    gfin_ref = refs.pop(0) if has_final else None
    o_ref, state_out_ref, h_ref, ext_ref, dbuf_ref, y_ref = refs
    hist, hpad = POOL_HIST, _round_up(POOL_HIST, SUBLANES)
    cc = POOL_GROUP_W
    pitch = _group_pitch(seg)
    spg = cc // LANES

    _init_slab_history(ext_ref, state_ref, hist, hpad, nseq, seg)
    _prenorm(x_ref, mod_ref, g_ref, h_ref, nseq, seg)
    h = h_ref[...]

    lane_row = lax.broadcasted_iota(jnp.int32, (SUBLANES, LANES), 0) * pitch

    for gidx, win in enumerate(POOL_WINDOWS):
        cols = slice(gidx * cc, (gidx + 1) * cc)
        pv = _dot(h, win_ref[:, cols])
        z = _dot(h, win_ref[:, D_INNER + gidx * cc:D_INNER + (gidx + 1) * cc])
        for s in range(nseq):
            for ci in range(spg):
                ext_ref[s, gidx * spg + ci, hpad:hpad + seg, :] = (
                    pv[s * seg:(s + 1) * seg, ci * LANES:(ci + 1) * LANES])
        if not has_state:
            for ci in range(spg):
                _link_history(ext_ref, functools.partial(_slab_rows, gidx * spg + ci), hist, hpad, nseq, seg)
        for s in range(nseq):
            pos0 = PAST_LEN if has_state else pl.program_id(1) * (nseq * seg) + s * seg

            def rows_at(ci, r, s=s):
                return ext_ref[s, gidx * spg + ci, pl.ds(hpad + r, SUBLANES, stride=pitch), :]

            tots = [None] * spg
            for v in range(pitch):
                inv_cnt = 1.0 / jnp.minimum(win, lane_row + (pos0 + v + 1)).astype(F32)
                for ci in range(spg):
                    cur = rows_at(ci, v)
                    if v == 0:
                        tots[ci] = cur
                        for d in range(1, win):
                            tots[ci] = tots[ci] + rows_at(ci, -d)
                    else:
                        tots[ci] = tots[ci] + cur - rows_at(ci, v - win)
                    dbuf_ref[s, ci, pl.ds(v, SUBLANES, stride=pitch), :] = tots[ci] * inv_cnt - cur
        for ci in range(spg):
            c = gidx * spg + ci
            _roll_history(ext_ref, state_out_ref, functools.partial(_slab_rows, c),
                          functools.partial(_slab_lanes, c), hist, hpad, nseq, seg, chained=not has_state)
        dlt = jnp.concatenate(
            [jnp.concatenate([dbuf_ref[s, ci, 0:seg, :] for ci in range(spg)], axis=1) for s in range(nseq)],
            axis=0)
        yp = _dot(dlt.astype(BF16), wpool_ref[gidx]) * scale_ref[:, cols]
        y_ref[:, cols] = (yp * _silu(z)).astype(BF16)

    _finish(x_ref, mod_ref, _dot(y_ref[...], wout_ref[...]), gfin_ref, o_ref, nseq, seg)


def _const_spec(arr):
    nd = arr.ndim
    return pl.BlockSpec(arr.shape, lambda b, t, _nd=nd: (0,) * _nd, pipeline_mode=pl.Buffered(1))


def _run_layer(body, name, x, mod, g, state, weights, g_final, hist, hpad, nseq, seg, n_batch_steps,
               n_tiles, chained, extra_scratch, emit_v=False, with_state_out=True, ext_shape=None, stage_y=True):
    rows_t = nseq * seg
    m_rows = x.shape[0]
    seq_blk = 1 if chained else nseq
    n_seq_total = n_batch_steps * seq_blk
    in_arrays = [x, mod, g]
    in_specs = [
        pl.BlockSpec((rows_t, D_MODEL), lambda b, t: (b * n_tiles + t, 0)),
        pl.BlockSpec((seq_blk, 3, D_MODEL), lambda b, t: (b, 0, 0)),
        _const_spec(g),
    ]
    if state is not None:
        in_arrays.append(state)
        in_specs.append(_const_spec(state))
    for w in weights:
        in_arrays.append(w)
        in_specs.append(_const_spec(w))
    if g_final is not None:
        in_arrays.append(g_final)
        in_specs.append(_const_spec(g_final))

    out_shapes = [jax.ShapeDtypeStruct((m_rows, D_MODEL), F32)]
    out_specs = [pl.BlockSpec((rows_t, D_MODEL), lambda b, t: (b * n_tiles + t, 0))]
    if with_state_out:
        out_shapes.append(jax.ShapeDtypeStruct((n_seq_total, hist, D_INNER), F32))
        out_specs.append(pl.BlockSpec((seq_blk, hist, D_INNER), lambda b, t: (b, 0, 0)))
    if emit_v:
        out_shapes.append(jax.ShapeDtypeStruct((m_rows, D_INNER), F32))
        out_specs.append(pl.BlockSpec((rows_t, D_INNER), lambda b, t: (b * n_tiles + t, 0)))

    scratch = [pltpu.VMEM((rows_t, D_MODEL), BF16)]
    if with_state_out:
        scratch.append(pltpu.VMEM(ext_shape or (nseq, hpad + seg, D_INNER), F32))
    scratch += extra_scratch
    if stage_y:
        scratch.append(pltpu.VMEM((rows_t, D_INNER), BF16))

    return pl.pallas_call(
        body,
        grid=(n_batch_steps, n_tiles),
        in_specs=in_specs,
        out_specs=out_specs,
        out_shape=out_shapes,
        scratch_shapes=scratch,
        compiler_params=pltpu.CompilerParams(
            dimension_semantics=("arbitrary", "arbitrary"), vmem_limit_bytes=VMEM_LIMIT_BYTES),
        name=name,
    )(*in_arrays)


def _pack_conv_weights(w):
    taps = w.shape[0]
    pairs = D_INNER // (2 * LANES)
    w = w.reshape(taps, pairs, 2, 1, LANES).transpose(1, 0, 2, 3, 4)
    w = jnp.broadcast_to(w, (pairs, taps, 2, SUBLANES, LANES))
    return w.reshape(pairs, taps, 2 * SUBLANES, LANES)


def _mix_matrices(w_s_a, b_s_a, seg):
    blk = MLP_BLOCK if seg >= MLP_BLOCK else seg
    cidx = jnp.arange(blk) // CHUNK
    mask = cidx[:, None] >= cidx[None, :]
    w = jnp.where(mask[None], w_s_a[:, :blk, :blk], jnp.zeros((), w_s_a.dtype))
    reps = MIX_ROWS // blk
    eye = jnp.eye(reps, dtype=w.dtype)
    wmix = jnp.einsum('rs,gij->grisj', eye, w).reshape(MLP_GROUPS, reps * blk, reps * blk)
    bmix = jnp.tile(b_s_a[:, :blk], (1, reps))[:, :, None]
    return wmix.astype(BF16), bmix.astype(F32)


def _trunk(x, mod, state_b, state_c, state_d, nseq, seg, n_batch_steps, n_tiles, emit_v, prm):
    has_state = state_b is not None
    rows_t = nseq * seg
    row = lambda v: v.reshape(1, -1)
    wmix, bmix = _mix_matrices(prm["w_s_a"], prm["b_s_a"], seg)
    common = dict(nseq=nseq, seg=seg, n_batch_steps=n_batch_steps, n_tiles=n_tiles, chained=not has_state)

    res = _run_layer(
        functools.partial(_layer_a_kernel, nseq, seg, emit_v, False), "layer_a",
        x, mod[0], row(prm["g_norm"][0]), None,
        [prm["wa_in"], row(prm["ln_a_g"]), row(prm["ln_a_b"]), wmix, bmix, prm["wa_out"]],
        None, 0, 0, extra_scratch=[pltpu.VMEM((rows_t, D_INNER), F32)], emit_v=emit_v,
        with_state_out=False, **common)
    x = res[0]
    v_rows = res[1] if emit_v else None

    hist, hpad = SCONV_W - 1, SUBLANES
    x, conv_b = _run_layer(
        functools.partial(_layer_b_kernel, nseq, seg, has_state, False), "layer_b",
        x, mod[1], row(prm["g_norm"][1]), state_b,
        [prm["wb_in"], prm["w_conv_b"], prm["wb_out"]],
        None, hist, hpad, extra_scratch=[], **common)

    hist, hpad = CCONV_W - 1, _round_up(CCONV_W - 1, SUBLANES)
    pitch = _group_pitch(seg)
    x, conv_c = _run_layer(
        functools.partial(_layer_c_kernel, nseq, seg, has_state, False), "layer_c",
        x, mod[2], row(prm["g_norm"][2]), state_c,
        [prm["wc_in"], prm["wc_conv"], prm["bc_conv"], row(prm["ln_c_g"]),
         row(prm["ln_c_b"]), prm["wc_out"]],
        None, hist, hpad,
        extra_scratch=[pltpu.VMEM((nseq, D_INNER // LANES, SUBLANES * pitch, LANES), F32),
                       pltpu.VMEM((rows_t, D_INNER), F32),
                       pltpu.VMEM((3, rows_t, COL_CHUNK), F32)],
        ext_shape=(nseq, D_INNER // LANES, hpad + SUBLANES * pitch, LANES), stage_y=False, **common)

    hist, hpad = POOL_HIST, _round_up(POOL_HIST, SUBLANES)
    y, pool_d = _run_layer(
        functools.partial(_layer_d_kernel, nseq, seg, has_state, True), "layer_d",
        x, mod[3], row(prm["g_norm"][3]), state_d,
        [prm["wd_in"], prm["wd_pool"], row(prm["scale_pool_d"]), prm["wd_out"]],
        row(prm["g_final"]), hist, hpad,
        extra_scratch=[pltpu.VMEM((nseq, POOL_GROUP_W // LANES, SUBLANES * pitch, LANES), F32)],
        ext_shape=(nseq, D_INNER // LANES, hpad + SUBLANES * pitch, LANES), **common)
    return y, v_rows, conv_b, conv_c, pool_d


def kernel(x_prompt, x_sample, state_conv_b, state_conv_c, state_pool_d, c_prompt, c_sample, w_ada, b_ada, g_norm, w_in_a, ln_a_g, ln_a_b, w_s_a, b_s_a, w_out_a, w_in_b, w_conv_b, w_out_b, w_in_c, w_conv_c, b_conv_c, ln_c_g, ln_c_b, w_out_c, w_in_d, w_pool_d, scale_pool_d, w_out_d, g_final):
    n_p, s_p, _ = x_prompt.shape
    n_s, s_s, _ = x_sample.shape

    c_all = jnp.concatenate([c_prompt, c_sample], axis=0)
    pad = _round_up(n_p + n_s, SUBLANES) - (n_p + n_s)
    c_all = jnp.pad(c_all, ((0, pad), (0, 0)))
    mod = _modulation(c_all, w_ada, b_ada)
    mod_p = mod[:, :n_p].reshape(DEPTH, n_p, 3, D_MODEL)
    mod_s = mod[:, n_p:n_p + n_s].reshape(DEPTH, n_s, 3, D_MODEL)

    prm = dict(
        g_norm=g_norm, ln_a_g=ln_a_g, ln_a_b=ln_a_b, w_s_a=w_s_a, b_s_a=b_s_a,
        w_conv_b=w_conv_b, ln_c_g=ln_c_g, ln_c_b=ln_c_b,
        scale_pool_d=scale_pool_d, g_final=g_final,
        wa_in=w_in_a.astype(BF16),
        wa_out=w_out_a.astype(BF16),
        wb_in=w_in_b.astype(BF16),
        wb_out=w_out_b.astype(BF16),
        wc_in=w_in_c.astype(BF16),
        wc_conv=_pack_conv_weights(w_conv_c).astype(BF16),
        bc_conv=_pack_conv_weights(b_conv_c.reshape(1, -1))[:, 0],
        wc_out=w_out_c.astype(BF16),
        wd_in=w_in_d.astype(BF16),
        wd_pool=w_pool_d.astype(BF16),
        wd_out=w_out_d.astype(BF16),
    )

    y_p, _, cb_p, cc_p, pd_p = _trunk(
        x_prompt.reshape(n_p * s_p, D_MODEL), mod_p, None, None, None,
        nseq=ROW_TILE // SEG_ROWS, seg=SEG_ROWS, n_batch_steps=n_p, n_tiles=s_p // ROW_TILE, emit_v=False, prm=prm)
    y_s, v_s, cb_s, cc_s, pd_s = _trunk(
        x_sample.reshape(n_s * s_s, D_MODEL), mod_s, state_conv_b, state_conv_c, state_pool_d,
        nseq=n_s, seg=s_s, n_batch_steps=1, n_tiles=1, emit_v=True, prm=prm)

    return (y_p.reshape(n_p, s_p, D_MODEL), y_s.reshape(n_s, s_s, D_MODEL),
            v_s.reshape(n_s, s_s, D_INNER), cb_p, cb_s, cc_p, cc_s, pd_p, pd_s)
```

```python
import functools

import jax
import jax.numpy as jnp
from jax import lax
from jax.experimental import pallas as pl
from jax.experimental.pallas import tpu as pltpu

D_MODEL = 1024
D_INNER = 2048
DEPTH = 4
CHUNK = 64
MLP_BLOCK = 128
MLP_GROUPS = 8
MLP_GROUP_W = D_INNER // MLP_GROUPS
SCONV_W = 3
CCONV_W = 31
POOL_WINDOWS = (2, 4, 8, 16)
POOL_GROUP_W = D_INNER // len(POOL_WINDOWS)
POOL_HIST = max(POOL_WINDOWS) - 1
PAST_LEN = 2048
EPS = 1e-6

F32 = jnp.float32
BF16 = jnp.bfloat16

SUBLANES = 8
LANES = 128
MXU_COLS = 256
VACC_UNITS = 4
CONV_BLOCK = 8
VMEM_LIMIT_BYTES = 58 * 1024 * 1024
SINGLE_BUFFER_BYTES = 2 * 1024 * 1024

ROW_TILE = 512
SEG_ROWS = 256
MIX_ROWS = 256
COL_CHUNK = 512
INV_SQRT2 = 0.7071067811865476


def _round_up(n, m):
    return (n + m - 1) // m * m


def _gelu(x):
    return 0.5 * x * (1.0 + lax.erf(x * INV_SQRT2))


def _silu(x):
    return x * jax.nn.sigmoid(x)


def _dot(a, b):
    return jnp.dot(a, b, preferred_element_type=F32)


def _mod_kernel(c_ref, w_ref, b_ref, o_ref):
    c = c_ref[...]
    a = _silu(c).astype(BF16)
    o_ref[0] = _dot(a, w_ref[0].astype(BF16)) + b_ref[0]


def _modulation(c_all, w_ada, b_ada):
    rows = c_all.shape[0]
    nb = 1024
    return pl.pallas_call(
        _mod_kernel,
        grid=(DEPTH, 3 * D_MODEL // nb),
        in_specs=[
            pl.BlockSpec((rows, D_MODEL), lambda i, j: (0, 0)),
            pl.BlockSpec((1, D_MODEL, nb), lambda i, j: (i, 0, j)),
            pl.BlockSpec((1, 1, nb), lambda i, j: (i, 0, j)),
        ],
        out_specs=pl.BlockSpec((1, rows, nb), lambda i, j: (i, 0, j)),
        out_shape=jax.ShapeDtypeStruct((DEPTH, rows, 3 * D_MODEL), F32),
        compiler_params=pltpu.CompilerParams(
            dimension_semantics=("arbitrary", "arbitrary"), vmem_limit_bytes=VMEM_LIMIT_BYTES),
        name="ada_modulation",
    )(c_all, w_ada, b_ada.reshape(DEPTH, 1, 3 * D_MODEL))


def _prenorm(x_ref, mod_ref, g_ref, h_ref, nseq, seg):
    g = g_ref[...]
    for s in range(nseq):
        rows = slice(s * seg, (s + 1) * seg)
        x = x_ref[rows, :]
        ms = jnp.mean(x * x, axis=-1, keepdims=True)
        y = (x * lax.rsqrt(ms + EPS)) * g
        m = s if mod_ref.shape[0] > 1 else 0
        h = y * (1.0 + mod_ref[m, 1:2, :]) + mod_ref[m, 0:1, :]
        h_ref[rows, :] = h.astype(BF16)


def _zero_of(value):
    word = lax.bitcast_convert_type(value, jnp.uint32)
    return lax.shift_right_logical(lax.shift_right_logical(word, jnp.uint32(16)), jnp.uint32(16))


def _finish(x_ref, mod_ref, out, gfin_ref, o_ref, nseq, seg):
    for s in range(nseq):
        rows = slice(s * seg, (s + 1) * seg)
        m = s if mod_ref.shape[0] > 1 else 0
        xn = x_ref[rows, :] + mod_ref[m, 2:3, :] * out[rows, :]
        if gfin_ref is not None:
            ms = jnp.mean(xn * xn, axis=-1, keepdims=True)
            xn = (xn * lax.rsqrt(ms + EPS)) * gfin_ref[...]
        o_ref[rows, :] = xn


def _init_history(ext_ref, state_ref, hist, hpad, nseq):
    if state_ref is None:
        @pl.when(pl.program_id(1) == 0)
        def _():
            ext_ref[:, 0:hpad, :] = jnp.zeros((nseq, hpad, D_INNER), F32)
    else:
        for s in range(nseq):
            ext_ref[s, hpad - hist:hpad, :] = state_ref[s]


def _init_slab_history(ext_ref, state_ref, hist, hpad, nseq, seg):
    _, n_slab, ext_rows, _ = ext_ref.shape
    tail = ext_rows - hpad - seg
    if state_ref is None:
        @pl.when(pl.program_id(1) == 0)
        def _():
            ext_ref[:, :, 0:hpad, :] = jnp.zeros((nseq, n_slab, hpad, LANES), F32)
            if tail:
                ext_ref[:, :, hpad + seg:ext_rows, :] = jnp.zeros((nseq, n_slab, tail, LANES), F32)
    else:
        for s in range(nseq):
            for c in range(n_slab):
                ext_ref[s, c, hpad - hist:hpad, :] = state_ref[s, :, c * LANES:(c + 1) * LANES]
        if tail:
            ext_ref[:, :, hpad + seg:ext_rows, :] = jnp.zeros((nseq, n_slab, tail, LANES), F32)


def _slab_rows(c, s, r0, r1):
    return (s, c, slice(r0, r1), slice(None))


def _slab_lanes(c, s):
    return (s, slice(None), slice(c * LANES, (c + 1) * LANES))


def _group_pitch(seg):
    pitch = seg // SUBLANES
    while pitch % 8 == 0:
        pitch += 1
    return pitch


def _link_history(ext_ref, at, hist, hpad, nseq, seg):
    for s in range(1, nseq):
        ext_ref[at(s, hpad - hist, hpad)] = ext_ref[at(s - 1, hpad + seg - hist, hpad + seg)]


def _roll_history(ext_ref, state_out_ref, at, out_at, hist, hpad, nseq, seg, chained):
    if chained:
        last = ext_ref[at(nseq - 1, hpad + seg - hist, hpad + seg)]
        state_out_ref[out_at(0)] = last
        ext_ref[at(0, hpad - hist, hpad)] = last
    else:
        for s in range(nseq):
            last = ext_ref[at(s, hpad + seg - hist, hpad + seg)]
            state_out_ref[out_at(s)] = last
            ext_ref[at(s, hpad - hist, hpad)] = last


def _layer_a_kernel(nseq, seg, emit_v, has_final, *refs):
    refs = list(refs)
    x_ref, mod_ref, g_ref, win_ref, lng_ref, lnb_ref, wmix_ref, bmix_ref, wout_ref = refs[:9]
    refs = refs[9:]
    gfin_ref = refs.pop(0) if has_final else None
    o_ref = refs.pop(0)
    v_out_ref = refs.pop(0) if emit_v else None
    h_ref, vbuf_ref, y_ref = refs
    rows_t = nseq * seg
    mix_rows = wmix_ref.shape[1]
    nch = D_INNER // COL_CHUNK

    _prenorm(x_ref, mod_ref, g_ref, h_ref, nseq, seg)
    h = h_ref[...]

    shift = s1 = s2 = None
    for j in range(nch):
        cols = slice(j * COL_CHUNK, (j + 1) * COL_CHUNK)
        gv = _gelu(_dot(h, win_ref[:, D_INNER + j * COL_CHUNK:D_INNER + (j + 1) * COL_CHUNK]))
        vbuf_ref[:, cols] = gv
        if j == 0:
            shift = jnp.mean(gv, axis=-1, keepdims=True)
        dv = gv - shift
        a1 = jnp.sum(dv, axis=-1, keepdims=True)
        a2 = jnp.sum(dv * dv, axis=-1, keepdims=True)
        s1 = a1 if s1 is None else s1 + a1
        s2 = a2 if s2 is None else s2 + a2
    m1 = s1 * (1.0 / D_INNER)
    mu = shift + m1
    var = s2 * (1.0 / D_INNER) - m1 * m1
    rstd = lax.rsqrt(var + EPS)

    gpc = COL_CHUNK // MLP_GROUP_W
    for j in range(nch):
        cols = slice(j * COL_CHUNK, (j + 1) * COL_CHUNK)
        vn = (vbuf_ref[:, cols] - mu) * rstd * lng_ref[:, cols] + lnb_ref[:, cols]
        if emit_v:
            v_out_ref[:, cols] = vn
        vnb = vn.astype(BF16)
        mixed = []
        for gi in range(gpc):
            g = j * gpc + gi
            gcols = slice(gi * MLP_GROUP_W, (gi + 1) * MLP_GROUP_W)
            mixed.append(jnp.concatenate(
                [_dot(wmix_ref[g], vnb[r0:r0 + mix_rows, gcols]) + bmix_ref[g]
                 for r0 in range(0, rows_t, mix_rows)], axis=0))
        mixed = jnp.concatenate(mixed, axis=1)
        u = _gelu(_dot(h, win_ref[:, cols]))
        z = _dot(h, win_ref[:, 2 * D_INNER + j * COL_CHUNK:2 * D_INNER + (j + 1) * COL_CHUNK])
        y_ref[:, cols] = (u * mixed * _silu(z)).astype(BF16)

    _finish(x_ref, mod_ref, _dot(y_ref[...], wout_ref[...]), gfin_ref, o_ref, nseq, seg)


def _layer_b_kernel(nseq, seg, has_state, has_final, *refs):
    refs = list(refs)
    x_ref, mod_ref, g_ref = refs[:3]
    refs = refs[3:]
    state_ref = refs.pop(0) if has_state else None
    win_ref, wconv_ref, wout_ref = refs[:3]
    refs = refs[3:]
    gfin_ref = refs.pop(0) if has_final else None
    o_ref, state_out_ref, h_ref, ext_ref, y_ref = refs
    hist, hpad = SCONV_W - 1, SUBLANES
    nch = D_INNER // COL_CHUNK
    cc = COL_CHUNK

    _prenorm(x_ref, mod_ref, g_ref, h_ref, nseq, seg)
    _init_history(ext_ref, state_ref, hist, hpad, nseq)
    h = h_ref[...]

    for j in range(nch):
        cols = slice(j * cc, (j + 1) * cc)
        bg, cg, hv, z = [
            _dot(h, win_ref[:, br * D_INNER + j * cc:br * D_INNER + (j + 1) * cc]) for br in range(4)]
        p = cg * hv
        gate = bg * _silu(z)
        at = lambda s, r0, r1, cols=cols: (s, slice(r0, r1), cols)
        for s in range(nseq):
            ext_ref[s, hpad:hpad + seg, cols] = p[s * seg:(s + 1) * seg, :]
        if not has_state:
            _link_history(ext_ref, at, hist, hpad, nseq, seg)
        for s in range(nseq):
            rows = slice(s * seg, (s + 1) * seg)
            conv = p[rows, :] * wconv_ref[2:3, cols]
            for k in range(SCONV_W - 1):
                d = SCONV_W - 1 - k
                conv = conv + ext_ref[s, hpad - d:hpad - d + seg, cols] * wconv_ref[k:k + 1, cols]
            y_ref[rows, cols] = (conv * gate[rows, :]).astype(BF16)
        _roll_history(ext_ref, state_out_ref, at, lambda s, cols=cols: (s, slice(None), cols),
                      hist, hpad, nseq, seg, chained=not has_state)

    _finish(x_ref, mod_ref, _dot(y_ref[...], wout_ref[...]), gfin_ref, o_ref, nseq, seg)


def _layer_c_kernel(nseq, seg, has_state, has_final, *refs):
    refs = list(refs)
    x_ref, mod_ref, g_ref = refs[:3]
    refs = refs[3:]
    state_ref = refs.pop(0) if has_state else None
    win_ref, wconv_ref, bconv_ref, lng_ref, lnb_ref, wout_ref = refs[:6]
    refs = refs[6:]
    gfin_ref = refs.pop(0) if has_final else None
    o_ref, state_out_ref, h_ref, ext_ref, cbuf_ref, zs_ref, pbuf_ref = refs
    hist, hpad = CCONV_W - 1, _round_up(CCONV_W - 1, SUBLANES)
    nch = D_INNER // COL_CHUNK
    cc = COL_CHUNK
    pitch = _group_pitch(seg)
    pairs_per_chunk = cc // LANES // 2

    _init_slab_history(ext_ref, state_ref, hist, hpad, nseq, seg)
    _prenorm(x_ref, mod_ref, g_ref, h_ref, nseq, seg)

    h = h_ref[...]

    def proj_piece(branch, j, piece):
        lo = piece * MXU_COLS
        w_lo = branch * D_INNER + j * cc + lo
        pbuf_ref[branch, :, lo:lo + MXU_COLS] = _dot(h, win_ref[:, w_lo:w_lo + MXU_COLS])

    def conv_unit(j, s, unit, carry):
        p = j * pairs_per_chunk + unit
        c0 = 2 * p
        bias = bconv_ref[p]
        prev = None
        srcs = {}
        for v0 in range(0, pitch, VACC_UNITS):
            n_acc = min(VACC_UNITS, pitch - v0)
            if v0 % CONV_BLOCK == 0:
                srcs = {}
                for u in range(v0, min(v0 + CONV_BLOCK, pitch) + CCONV_W - 1):
                    start = hpad - (CCONV_W - 1) + u
                    srcs[u] = jnp.concatenate(
                        [ext_ref[s, c0, pl.ds(start, SUBLANES, stride=pitch), :],
                         ext_ref[s, c0 + 1, pl.ds(start, SUBLANES, stride=pitch), :]], axis=0).astype(BF16)
            accs = [None] * n_acc
            for k in range(CCONV_W):
                wk = wconv_ref[p, k]
                if k == 0 and prev is not None:
                    wk = pltpu.bitcast(pltpu.bitcast(wk, jnp.uint32) | _zero_of(prev), BF16)
                wk = wk.astype(F32)
                for i in range(n_acc):
                    term = srcs[v0 + i + k].astype(F32) * wk
                    accs[i] = term if accs[i] is None else accs[i] + term
            prev = accs[n_acc - 1][0:SUBLANES]
            for i in range(n_acc):
                res = accs[i] + bias
                cbuf_ref[s, c0, pl.ds(v0 + i, SUBLANES, stride=pitch), :] = res[0:SUBLANES]
                cbuf_ref[s, c0 + 1, pl.ds(v0 + i, SUBLANES, stride=pitch), :] = res[SUBLANES:2 * SUBLANES]
        return carry

    def read_conv(j):
        return jnp.concatenate(
            [jnp.concatenate([cbuf_ref[s, j * (cc // LANES) + ci, 0:seg, :] for ci in range(cc // LANES)], axis=1)
             for s in range(nseq)], axis=0)

    shift = s1 = s2 = None
    n_piece = cc // MXU_COLS
    for piece in range(n_piece):
        proj_piece(0, 0, piece)
        proj_piece(1, 0, piece)
    for j in range(nch):
        glu = pbuf_ref[0] * jax.nn.sigmoid(pbuf_ref[1])
        for s in range(nseq):
            for ci in range(cc // LANES):
                ext_ref[s, j * (cc // LANES) + ci, hpad:hpad + seg, :] = (
                    glu[s * seg:(s + 1) * seg, ci * LANES:(ci + 1) * LANES])
        if not has_state:
            for ci in range(cc // LANES):
                _link_history(ext_ref, functools.partial(_slab_rows, j * (cc // LANES) + ci), hist, hpad, nseq, seg)
        for piece in range(n_piece):
            proj_piece(2, j, piece)
        for s in range(nseq):
            lax.fori_loop(0, pairs_per_chunk, functools.partial(conv_unit, j, s), 0)
        for ci in range(cc // LANES):
            c = j * (cc // LANES) + ci
            _roll_history(ext_ref, state_out_ref, functools.partial(_slab_rows, c),
                          functools.partial(_slab_lanes, c), hist, hpad, nseq, seg, chained=not has_state)
        zs_ref[:, j * cc:(j + 1) * cc] = _silu(pbuf_ref[2])
        if j + 1 < nch:
            for piece in range(n_piece):
                proj_piece(0, j + 1, piece)
                proj_piece(1, j + 1, piece)
        cv = read_conv(j)
        if j == 0:
            shift = jnp.mean(cv, axis=-1, keepdims=True)
        dv = cv - shift
        a1 = jnp.sum(dv, axis=-1, keepdims=True)
        a2 = jnp.sum(dv * dv, axis=-1, keepdims=True)
        s1 = a1 if s1 is None else s1 + a1
        s2 = a2 if s2 is None else s2 + a2
    m1 = s1 * (1.0 / D_INNER)
    mu = shift + m1
    var = s2 * (1.0 / D_INNER) - m1 * m1
    rstd = lax.rsqrt(var + EPS)

    out = None
    for j in range(nch):
        cols = slice(j * cc, (j + 1) * cc)
        yn = (read_conv(j) - mu) * rstd * lng_ref[:, cols] + lnb_ref[:, cols]
        part = _dot((_silu(yn) * zs_ref[:, cols]).astype(BF16), wout_ref[cols, :])
        out = part if out is None else out + part

    _finish(x_ref, mod_ref, out, gfin_ref, o_ref, nseq, seg)


def _layer_d_kernel(nseq, seg, has_state, has_final, *refs):
    refs = list(refs)
    x_ref, mod_ref, g_ref = refs[:3]
    refs = refs[3:]
    state_ref = refs.pop(0) if has_state else None
    win_ref, wpool_ref, scale_ref, wout_ref = refs[:4]
    refs = refs[4:]
    gfin_ref = refs.pop(0) if has_final else None
    o_ref, state_out_ref, h_ref, ext_ref, dbuf_ref, y_ref = refs
    hist, hpad = POOL_HIST, _round_up(POOL_HIST, SUBLANES)
    cc = POOL_GROUP_W
    pitch = _group_pitch(seg)
    spg = cc // LANES

    _init_slab_history(ext_ref, state_ref, hist, hpad, nseq, seg)
    _prenorm(x_ref, mod_ref, g_ref, h_ref, nseq, seg)
    h = h_ref[...]

    lane_row = lax.broadcasted_iota(jnp.int32, (SUBLANES, LANES), 0) * pitch

    for gidx, win in enumerate(POOL_WINDOWS):
        cols = slice(gidx * cc, (gidx + 1) * cc)
        pv = _dot(h, win_ref[:, cols])
        z = _dot(h, win_ref[:, D_INNER + gidx * cc:D_INNER + (gidx + 1) * cc])
        for s in range(nseq):
            for ci in range(spg):
                ext_ref[s, gidx * spg + ci, hpad:hpad + seg, :] = (
                    pv[s * seg:(s + 1) * seg, ci * LANES:(ci + 1) * LANES])
        if not has_state:
            for ci in range(spg):
                _link_history(ext_ref, functools.partial(_slab_rows, gidx * spg + ci), hist, hpad, nseq, seg)
        for s in range(nseq):
            pos0 = PAST_LEN if has_state else pl.program_id(1) * (nseq * seg) + s * seg

            def rows_at(ci, r, s=s):
                return ext_ref[s, gidx * spg + ci, pl.ds(hpad + r, SUBLANES, stride=pitch), :]

            tots = [None] * spg
            for v in range(pitch):
                inv_cnt = 1.0 / jnp.minimum(win, lane_row + (pos0 + v + 1)).astype(F32)
                for ci in range(spg):
                    cur = rows_at(ci, v)
                    if v == 0:
                        tots[ci] = cur
                        for d in range(1, win):
                            tots[ci] = tots[ci] + rows_at(ci, -d)
                    else:
                        tots[ci] = tots[ci] + cur - rows_at(ci, v - win)
                    dbuf_ref[s, ci, pl.ds(v, SUBLANES, stride=pitch), :] = tots[ci] * inv_cnt - cur
        for ci in range(spg):
            c = gidx * spg + ci
            _roll_history(ext_ref, state_out_ref, functools.partial(_slab_rows, c),
                          functools.partial(_slab_lanes, c), hist, hpad, nseq, seg, chained=not has_state)
        dlt = jnp.concatenate(
            [jnp.concatenate([dbuf_ref[s, ci, 0:seg, :] for ci in range(spg)], axis=1) for s in range(nseq)],
            axis=0)
        yp = _dot(dlt.astype(BF16), wpool_ref[gidx]) * scale_ref[:, cols]
        y_ref[:, cols] = (yp * _silu(z)).astype(BF16)

    _finish(x_ref, mod_ref, _dot(y_ref[...], wout_ref[...]), gfin_ref, o_ref, nseq, seg)


def _const_spec(arr):
    nd = arr.ndim
    index_map = lambda b, t, _nd=nd: (0,) * _nd
    if arr.size * arr.dtype.itemsize >= SINGLE_BUFFER_BYTES:
        return pl.BlockSpec(arr.shape, index_map, pipeline_mode=pl.Buffered(1))
    return pl.BlockSpec(arr.shape, index_map)


def _run_layer(body, name, x, mod, g, state, weights, g_final, hist, hpad, nseq, seg, n_batch_steps,
               n_tiles, chained, extra_scratch, emit_v=False, with_state_out=True, ext_shape=None, stage_y=True):
    rows_t = nseq * seg
    m_rows = x.shape[0]
    seq_blk = 1 if chained else nseq
    n_seq_total = n_batch_steps * seq_blk
    in_arrays = [x, mod, g]
    in_specs = [
        pl.BlockSpec((rows_t, D_MODEL), lambda b, t: (b * n_tiles + t, 0)),
        pl.BlockSpec((seq_blk, 3, D_MODEL), lambda b, t: (b, 0, 0)),
        _const_spec(g),
    ]
    if state is not None:
        in_arrays.append(state)
        in_specs.append(_const_spec(state))
    for w in weights:
        in_arrays.append(w)
        in_specs.append(_const_spec(w))
    if g_final is not None:
        in_arrays.append(g_final)
        in_specs.append(_const_spec(g_final))

    out_shapes = [jax.ShapeDtypeStruct((m_rows, D_MODEL), F32)]
    out_specs = [pl.BlockSpec((rows_t, D_MODEL), lambda b, t: (b * n_tiles + t, 0))]
    if with_state_out:
        out_shapes.append(jax.ShapeDtypeStruct((n_seq_total, hist, D_INNER), F32))
        out_specs.append(pl.BlockSpec((seq_blk, hist, D_INNER), lambda b, t: (b, 0, 0)))
    if emit_v:
        out_shapes.append(jax.ShapeDtypeStruct((m_rows, D_INNER), F32))
        out_specs.append(pl.BlockSpec((rows_t, D_INNER), lambda b, t: (b * n_tiles + t, 0)))

    scratch = [pltpu.VMEM((rows_t, D_MODEL), BF16)]
    if with_state_out:
        scratch.append(pltpu.VMEM(ext_shape or (nseq, hpad + seg, D_INNER), F32))
    scratch += extra_scratch
    if stage_y:
        scratch.append(pltpu.VMEM((rows_t, D_INNER), BF16))

    return pl.pallas_call(
        body,
        grid=(n_batch_steps, n_tiles),
        in_specs=in_specs,
        out_specs=out_specs,
        out_shape=out_shapes,
        scratch_shapes=scratch,
        compiler_params=pltpu.CompilerParams(
            dimension_semantics=("arbitrary", "arbitrary"), vmem_limit_bytes=VMEM_LIMIT_BYTES),
        name=name,
    )(*in_arrays)


def _pack_conv_weights(w):
    taps = w.shape[0]
    pairs = D_INNER // (2 * LANES)
    w = w.reshape(taps, pairs, 2, 1, LANES).transpose(1, 0, 2, 3, 4)
    w = jnp.broadcast_to(w, (pairs, taps, 2, SUBLANES, LANES))
    return w.reshape(pairs, taps, 2 * SUBLANES, LANES)


def _mix_matrices(w_s_a, b_s_a, seg):
    blk = MLP_BLOCK if seg >= MLP_BLOCK else seg
    cidx = jnp.arange(blk) // CHUNK
    mask = cidx[:, None] >= cidx[None, :]
    w = jnp.where(mask[None], w_s_a[:, :blk, :blk], jnp.zeros((), w_s_a.dtype))
    reps = MIX_ROWS // blk
    eye = jnp.eye(reps, dtype=w.dtype)
    wmix = jnp.einsum('rs,gij->grisj', eye, w).reshape(MLP_GROUPS, reps * blk, reps * blk)
    bmix = jnp.tile(b_s_a[:, :blk], (1, reps))[:, :, None]
    return wmix.astype(BF16), bmix.astype(F32)


def _trunk(x, mod, state_b, state_c, state_d, nseq, seg, n_batch_steps, n_tiles, emit_v, prm):
    has_state = state_b is not None
    rows_t = nseq * seg
    row = lambda v: v.reshape(1, -1)
    wmix, bmix = _mix_matrices(prm["w_s_a"], prm["b_s_a"], seg)
    common = dict(nseq=nseq, seg=seg, n_batch_steps=n_batch_steps, n_tiles=n_tiles, chained=not has_state)

    res = _run_layer(
        functools.partial(_layer_a_kernel, nseq, seg, emit_v, False), "layer_a",
        x, mod[0], row(prm["g_norm"][0]), None,
        [prm["wa_in"], row(prm["ln_a_g"]), row(prm["ln_a_b"]), wmix, bmix, prm["wa_out"]],
        None, 0, 0, extra_scratch=[pltpu.VMEM((rows_t, D_INNER), F32)], emit_v=emit_v,
        with_state_out=False, **common)
    x = res[0]
    v_rows = res[1] if emit_v else None

    hist, hpad = SCONV_W - 1, SUBLANES
    x, conv_b = _run_layer(
        functools.partial(_layer_b_kernel, nseq, seg, has_state, False), "layer_b",
        x, mod[1], row(prm["g_norm"][1]), state_b,
        [prm["wb_in"], prm["w_conv_b"], prm["wb_out"]],
        None, hist, hpad, extra_scratch=[], **common)

    hist, hpad = CCONV_W - 1, _round_up(CCONV_W - 1, SUBLANES)
    pitch = _group_pitch(seg)
    x, conv_c = _run_layer(
        functools.partial(_layer_c_kernel, nseq, seg, has_state, False), "layer_c",
        x, mod[2], row(prm["g_norm"][2]), state_c,
        [prm["wc_in"], prm["wc_conv"], prm["bc_conv"], row(prm["ln_c_g"]),
         row(prm["ln_c_b"]), prm["wc_out"]],
        None, hist, hpad,
        extra_scratch=[pltpu.VMEM((nseq, D_INNER // LANES, SUBLANES * pitch, LANES), F32),
                       pltpu.VMEM((rows_t, D_INNER), F32),
                       pltpu.VMEM((3, rows_t, COL_CHUNK), F32)],
        ext_shape=(nseq, D_INNER // LANES, hpad + SUBLANES * pitch, LANES), stage_y=False, **common)

    hist, hpad = POOL_HIST, _round_up(POOL_HIST, SUBLANES)
    y, pool_d = _run_layer(
        functools.partial(_layer_d_kernel, nseq, seg, has_state, True), "layer_d",
        x, mod[3], row(prm["g_norm"][3]), state_d,
        [prm["wd_in"], prm["wd_pool"], row(prm["scale_pool_d"]), prm["wd_out"]],
        row(prm["g_final"]), hist, hpad,
        extra_scratch=[pltpu.VMEM((nseq, POOL_GROUP_W // LANES, SUBLANES * pitch, LANES), F32)],
        ext_shape=(nseq, D_INNER // LANES, hpad + SUBLANES * pitch, LANES), **common)
    return y, v_rows, conv_b, conv_c, pool_d


def kernel(x_prompt, x_sample, state_conv_b, state_conv_c, state_pool_d, c_prompt, c_sample, w_ada, b_ada, g_norm, w_in_a, ln_a_g, ln_a_b, w_s_a, b_s_a, w_out_a, w_in_b, w_conv_b, w_out_b, w_in_c, w_conv_c, b_conv_c, ln_c_g, ln_c_b, w_out_c, w_in_d, w_pool_d, scale_pool_d, w_out_d, g_final):
    n_p, s_p, _ = x_prompt.shape
    n_s, s_s, _ = x_sample.shape

    c_all = jnp.concatenate([c_prompt, c_sample], axis=0)
    pad = _round_up(n_p + n_s, SUBLANES) - (n_p + n_s)
    c_all = jnp.pad(c_all, ((0, pad), (0, 0)))
    mod = _modulation(c_all, w_ada, b_ada)
    mod_p = mod[:, :n_p].reshape(DEPTH, n_p, 3, D_MODEL)
    mod_s = mod[:, n_p:n_p + n_s].reshape(DEPTH, n_s, 3, D_MODEL)

    prm = dict(
        g_norm=g_norm, ln_a_g=ln_a_g, ln_a_b=ln_a_b, w_s_a=w_s_a, b_s_a=b_s_a,
        w_conv_b=w_conv_b, ln_c_g=ln_c_g, ln_c_b=ln_c_b,
        scale_pool_d=scale_pool_d, g_final=g_final,
        wa_in=w_in_a.astype(BF16),
        wa_out=w_out_a.astype(BF16),
        wb_in=w_in_b.astype(BF16),
        wb_out=w_out_b.astype(BF16),
        wc_in=w_in_c.astype(BF16),
        wc_conv=_pack_conv_weights(w_conv_c).astype(BF16),
        bc_conv=_pack_conv_weights(b_conv_c.reshape(1, -1))[:, 0],
        wc_out=w_out_c.astype(BF16),
        wd_in=w_in_d.astype(BF16),
        wd_pool=w_pool_d.astype(BF16),
        wd_out=w_out_d.astype(BF16),
    )

    y_p, _, cb_p, cc_p, pd_p = _trunk(
        x_prompt.reshape(n_p * s_p, D_MODEL), mod_p, None, None, None,
        nseq=ROW_TILE // SEG_ROWS, seg=SEG_ROWS, n_batch_steps=n_p, n_tiles=s_p // ROW_TILE, emit_v=False, prm=prm)
    y_s, v_s, cb_s, cc_s, pd_s = _trunk(
        x_sample.reshape(n_s * s_s, D_MODEL), mod_s, state_conv_b, state_conv_c, state_pool_d,
        nseq=n_s, seg=s_s, n_batch_steps=1, n_tiles=1, emit_v=True, prm=prm)

    return (y_p.reshape(n_p, s_p, D_MODEL), y_s.reshape(n_s, s_s, D_MODEL),
            v_s.reshape(n_s, s_s, D_INNER), cb_p, cb_s, cc_p, cc_s, pd_p, pd_s)
```

```python
import functools

import jax
import jax.numpy as jnp
from jax import lax
from jax.experimental import pallas as pl
from jax.experimental.pallas import tpu as pltpu

D_MODEL = 1024
D_INNER = 2048
DEPTH = 4
CHUNK = 64
MLP_BLOCK = 128
MLP_GROUPS = 8
MLP_GROUP_W = D_INNER // MLP_GROUPS
SCONV_W = 3
CCONV_W = 31
POOL_WINDOWS = (2, 4, 8, 16)
POOL_GROUP_W = D_INNER // len(POOL_WINDOWS)
POOL_HIST = max(POOL_WINDOWS) - 1
PAST_LEN = 2048
EPS = 1e-6

F32 = jnp.float32
BF16 = jnp.bfloat16

SUBLANES = 8
LANES = 128
MXU_COLS = 256
VACC_UNITS = 4
CONV_BLOCK = 8
VMEM_LIMIT_BYTES = 58 * 1024 * 1024

ROW_TILE = 512
SEG_ROWS = 256
MIX_ROWS = 256
COL_CHUNK = 512
INV_SQRT2 = 0.7071067811865476


def _round_up(n, m):
    return (n + m - 1) // m * m


def _gelu(x):
    return 0.5 * x * (1.0 + lax.erf(x * INV_SQRT2))


def _silu(x):
    return x * jax.nn.sigmoid(x)


def _dot(a, b):
    return jnp.dot(a, b, preferred_element_type=F32)


def _mod_kernel(c_ref, w_ref, b_ref, o_ref):
    c = c_ref[...]
    a = _silu(c).astype(BF16)
    o_ref[0] = _dot(a, w_ref[0].astype(BF16)) + b_ref[0]


def _modulation(c_all, w_ada, b_ada):
    rows = c_all.shape[0]
    nb = 1024
    return pl.pallas_call(
        _mod_kernel,
        grid=(DEPTH, 3 * D_MODEL // nb),
        in_specs=[
            pl.BlockSpec((rows, D_MODEL), lambda i, j: (0, 0)),
            pl.BlockSpec((1, D_MODEL, nb), lambda i, j: (i, 0, j)),
            pl.BlockSpec((1, 1, nb), lambda i, j: (i, 0, j)),
        ],
        out_specs=pl.BlockSpec((1, rows, nb), lambda i, j: (i, 0, j)),
        out_shape=jax.ShapeDtypeStruct((DEPTH, rows, 3 * D_MODEL), F32),
        compiler_params=pltpu.CompilerParams(
            dimension_semantics=("arbitrary", "arbitrary"), vmem_limit_bytes=VMEM_LIMIT_BYTES),
        name="ada_modulation",
    )(c_all, w_ada, b_ada.reshape(DEPTH, 1, 3 * D_MODEL))


def _prenorm(x_ref, mod_ref, g_ref, h_ref, nseq, seg):
    g = g_ref[...]
    for s in range(nseq):
        rows = slice(s * seg, (s + 1) * seg)
        x = x_ref[rows, :]
        ms = jnp.mean(x * x, axis=-1, keepdims=True)
        y = (x * lax.rsqrt(ms + EPS)) * g
        m = s if mod_ref.shape[0] > 1 else 0
        h = y * (1.0 + mod_ref[m, 1:2, :]) + mod_ref[m, 0:1, :]
        h_ref[rows, :] = h.astype(BF16)


def _zero_of(value):
    word = lax.bitcast_convert_type(value, jnp.uint32)
    return lax.shift_right_logical(lax.shift_right_logical(word, jnp.uint32(16)), jnp.uint32(16))


def _finish(x_ref, mod_ref, out, gfin_ref, o_ref, nseq, seg):
    for s in range(nseq):
        rows = slice(s * seg, (s + 1) * seg)
        m = s if mod_ref.shape[0] > 1 else 0
        xn = x_ref[rows, :] + mod_ref[m, 2:3, :] * out[rows, :]
        if gfin_ref is not None:
            ms = jnp.mean(xn * xn, axis=-1, keepdims=True)
            xn = (xn * lax.rsqrt(ms + EPS)) * gfin_ref[...]
        o_ref[rows, :] = xn


def _init_history(ext_ref, state_ref, hist, hpad, nseq):
    if state_ref is None:
        @pl.when(pl.program_id(1) == 0)
        def _():
            ext_ref[:, 0:hpad, :] = jnp.zeros((nseq, hpad, D_INNER), F32)
    else:
        for s in range(nseq):
            ext_ref[s, hpad - hist:hpad, :] = state_ref[s]


def _init_slab_history(ext_ref, state_ref, hist, hpad, nseq, seg):
    _, n_slab, ext_rows, _ = ext_ref.shape
    tail = ext_rows - hpad - seg
    if state_ref is None:
        @pl.when(pl.program_id(1) == 0)
        def _():
            ext_ref[:, :, 0:hpad, :] = jnp.zeros((nseq, n_slab, hpad, LANES), F32)
            if tail:
                ext_ref[:, :, hpad + seg:ext_rows, :] = jnp.zeros((nseq, n_slab, tail, LANES), F32)
    else:
        for s in range(nseq):
            for c in range(n_slab):
                ext_ref[s, c, hpad - hist:hpad, :] = state_ref[s, :, c * LANES:(c + 1) * LANES]
        if tail:
            ext_ref[:, :, hpad + seg:ext_rows, :] = jnp.zeros((nseq, n_slab, tail, LANES), F32)


def _slab_rows(c, s, r0, r1):
    return (s, c, slice(r0, r1), slice(None))


def _slab_lanes(c, s):
    return (s, slice(None), slice(c * LANES, (c + 1) * LANES))


def _group_pitch(seg):
    pitch = seg // SUBLANES
    while pitch % 8 == 0:
        pitch += 1
    return pitch


def _link_history(ext_ref, at, hist, hpad, nseq, seg):
    for s in range(1, nseq):
        ext_ref[at(s, hpad - hist, hpad)] = ext_ref[at(s - 1, hpad + seg - hist, hpad + seg)]


def _roll_history(ext_ref, state_out_ref, at, out_at, hist, hpad, nseq, seg, chained):
    if chained:
        last = ext_ref[at(nseq - 1, hpad + seg - hist, hpad + seg)]
        state_out_ref[out_at(0)] = last
        ext_ref[at(0, hpad - hist, hpad)] = last
    else:
        for s in range(nseq):
            last = ext_ref[at(s, hpad + seg - hist, hpad + seg)]
            state_out_ref[out_at(s)] = last
            ext_ref[at(s, hpad - hist, hpad)] = last


def _layer_a_kernel(nseq, seg, emit_v, has_final, *refs):
    refs = list(refs)
    x_ref, mod_ref, g_ref, win_ref, lng_ref, lnb_ref, wmix_ref, bmix_ref, wout_ref = refs[:9]
    refs = refs[9:]
    gfin_ref = refs.pop(0) if has_final else None
    o_ref = refs.pop(0)
    v_out_ref = refs.pop(0) if emit_v else None
    h_ref, vbuf_ref, zs_ref, y_ref = refs
    rows_t = nseq * seg
    mix_rows = wmix_ref.shape[1]
    nch = D_INNER // COL_CHUNK

    _prenorm(x_ref, mod_ref, g_ref, h_ref, nseq, seg)
    h = h_ref[...]

    shift = s1 = s2 = None
    for j in range(nch):
        cols = slice(j * COL_CHUNK, (j + 1) * COL_CHUNK)
        gv = _gelu(_dot(h, win_ref[:, D_INNER + j * COL_CHUNK:D_INNER + (j + 1) * COL_CHUNK]))
        vbuf_ref[:, cols] = gv
        zs_ref[:, cols] = _silu(_dot(h, win_ref[:, 2 * D_INNER + j * COL_CHUNK:2 * D_INNER + (j + 1) * COL_CHUNK]))
        if j == 0:
            shift = jnp.mean(gv, axis=-1, keepdims=True)
        dv = gv - shift
        a1 = jnp.sum(dv, axis=-1, keepdims=True)
        a2 = jnp.sum(dv * dv, axis=-1, keepdims=True)
        s1 = a1 if s1 is None else s1 + a1
        s2 = a2 if s2 is None else s2 + a2
    m1 = s1 * (1.0 / D_INNER)
    mu = shift + m1
    var = s2 * (1.0 / D_INNER) - m1 * m1
    rstd = lax.rsqrt(var + EPS)

    gpc = COL_CHUNK // MLP_GROUP_W
    for j in range(nch):
        cols = slice(j * COL_CHUNK, (j + 1) * COL_CHUNK)
        u_raw = _dot(h, win_ref[:, cols])
        vn = (vbuf_ref[:, cols] - mu) * rstd * lng_ref[:, cols] + lnb_ref[:, cols]
        if emit_v:
            v_out_ref[:, cols] = vn
        vnb = vn.astype(BF16)
        mixed = []
        for gi in range(gpc):
            g = j * gpc + gi
            gcols = slice(gi * MLP_GROUP_W, (gi + 1) * MLP_GROUP_W)
            mixed.append(jnp.concatenate(
                [_dot(wmix_ref[g], vnb[r0:r0 + mix_rows, gcols]) + bmix_ref[g]
                 for r0 in range(0, rows_t, mix_rows)], axis=0))
        mixed = jnp.concatenate(mixed, axis=1)
        y_ref[:, cols] = (_gelu(u_raw) * mixed * zs_ref[:, cols]).astype(BF16)

    _finish(x_ref, mod_ref, _dot(y_ref[...], wout_ref[...]), gfin_ref, o_ref, nseq, seg)


def _layer_b_kernel(nseq, seg, has_state, has_final, *refs):
    refs = list(refs)
    x_ref, mod_ref, g_ref = refs[:3]
    refs = refs[3:]
    state_ref = refs.pop(0) if has_state else None
    win_ref, wconv_ref, wout_ref = refs[:3]
    refs = refs[3:]
    gfin_ref = refs.pop(0) if has_final else None
    o_ref, state_out_ref, h_ref, ext_ref, y_ref = refs
    hist, hpad = SCONV_W - 1, SUBLANES
    nch = D_INNER // COL_CHUNK
    cc = COL_CHUNK

    _prenorm(x_ref, mod_ref, g_ref, h_ref, nseq, seg)
    _init_history(ext_ref, state_ref, hist, hpad, nseq)
    h = h_ref[...]

    for j in range(nch):
        cols = slice(j * cc, (j + 1) * cc)
        bg, cg, hv, z = [
            _dot(h, win_ref[:, br * D_INNER + j * cc:br * D_INNER + (j + 1) * cc]) for br in range(4)]
        p = cg * hv
        gate = bg * _silu(z)
        at = lambda s, r0, r1, cols=cols: (s, slice(r0, r1), cols)
        for s in range(nseq):
            ext_ref[s, hpad:hpad + seg, cols] = p[s * seg:(s + 1) * seg, :]
        if not has_state:
            _link_history(ext_ref, at, hist, hpad, nseq, seg)
        for s in range(nseq):
            rows = slice(s * seg, (s + 1) * seg)
            conv = p[rows, :] * wconv_ref[2:3, cols]
            for k in range(SCONV_W - 1):
                d = SCONV_W - 1 - k
                conv = conv + ext_ref[s, hpad - d:hpad - d + seg, cols] * wconv_ref[k:k + 1, cols]
            y_ref[rows, cols] = (conv * gate[rows, :]).astype(BF16)
        _roll_history(ext_ref, state_out_ref, at, lambda s, cols=cols: (s, slice(None), cols),
                      hist, hpad, nseq, seg, chained=not has_state)

    _finish(x_ref, mod_ref, _dot(y_ref[...], wout_ref[...]), gfin_ref, o_ref, nseq, seg)


def _layer_c_kernel(nseq, seg, has_state, has_final, *refs):
    refs = list(refs)
    x_ref, mod_ref, g_ref = refs[:3]
    refs = refs[3:]
    state_ref = refs.pop(0) if has_state else None
    win_ref, wconv_ref, bconv_ref, lng_ref, lnb_ref, wout_ref = refs[:6]
    refs = refs[6:]
    gfin_ref = refs.pop(0) if has_final else None
    o_ref, state_out_ref, h_ref, ext_ref, cbuf_ref, zs_ref, pbuf_ref = refs
    hist, hpad = CCONV_W - 1, _round_up(CCONV_W - 1, SUBLANES)
    nch = D_INNER // COL_CHUNK
    cc = COL_CHUNK
    pitch = _group_pitch(seg)
    pairs_per_chunk = cc // LANES // 2

    _init_slab_history(ext_ref, state_ref, hist, hpad, nseq, seg)
    _prenorm(x_ref, mod_ref, g_ref, h_ref, nseq, seg)

    h = h_ref[...]

    def proj_piece(branch, j, piece):
        lo = piece * MXU_COLS
        w_lo = branch * D_INNER + j * cc + lo
        pbuf_ref[branch, :, lo:lo + MXU_COLS] = _dot(h, win_ref[:, w_lo:w_lo + MXU_COLS])

    def conv_unit(j, s, unit, carry):
        p = j * pairs_per_chunk + unit
        c0 = 2 * p
        bias = bconv_ref[p]
        prev = None
        srcs = {}
        for v0 in range(0, pitch, VACC_UNITS):
            n_acc = min(VACC_UNITS, pitch - v0)
            if v0 % CONV_BLOCK == 0:
                srcs = {}
                for u in range(v0, min(v0 + CONV_BLOCK, pitch) + CCONV_W - 1):
                    start = hpad - (CCONV_W - 1) + u
                    srcs[u] = jnp.concatenate(
                        [ext_ref[s, c0, pl.ds(start, SUBLANES, stride=pitch), :],
                         ext_ref[s, c0 + 1, pl.ds(start, SUBLANES, stride=pitch), :]], axis=0).astype(BF16)
            accs = [None] * n_acc
            for k in range(CCONV_W):
                wk = wconv_ref[p, k]
                if k == 0 and prev is not None:
                    wk = pltpu.bitcast(pltpu.bitcast(wk, jnp.uint32) | _zero_of(prev), BF16)
                wk = wk.astype(F32)
                for i in range(n_acc):
                    term = srcs[v0 + i + k].astype(F32) * wk
                    accs[i] = term if accs[i] is None else accs[i] + term
            prev = accs[n_acc - 1][0:SUBLANES]
            for i in range(n_acc):
                res = accs[i] + bias
                cbuf_ref[s, c0, pl.ds(v0 + i, SUBLANES, stride=pitch), :] = res[0:SUBLANES]
                cbuf_ref[s, c0 + 1, pl.ds(v0 + i, SUBLANES, stride=pitch), :] = res[SUBLANES:2 * SUBLANES]
        return carry

    def read_conv(j):
        return jnp.concatenate(
            [jnp.concatenate([cbuf_ref[s, j * (cc // LANES) + ci, 0:seg, :] for ci in range(cc // LANES)], axis=1)
             for s in range(nseq)], axis=0)

    shift = s1 = s2 = None
    n_piece = cc // MXU_COLS
    for piece in range(n_piece):
        proj_piece(0, 0, piece)
        proj_piece(1, 0, piece)
    for j in range(nch):
        glu = pbuf_ref[0] * jax.nn.sigmoid(pbuf_ref[1])
        for s in range(nseq):
            for ci in range(cc // LANES):
                ext_ref[s, j * (cc // LANES) + ci, hpad:hpad + seg, :] = (
                    glu[s * seg:(s + 1) * seg, ci * LANES:(ci + 1) * LANES])
        if not has_state:
            for ci in range(cc // LANES):
                _link_history(ext_ref, functools.partial(_slab_rows, j * (cc // LANES) + ci), hist, hpad, nseq, seg)
        for piece in range(n_piece):
            proj_piece(2, j, piece)
        for s in range(nseq):
            lax.fori_loop(0, pairs_per_chunk, functools.partial(conv_unit, j, s), 0)
        for ci in range(cc // LANES):
            c = j * (cc // LANES) + ci
            _roll_history(ext_ref, state_out_ref, functools.partial(_slab_rows, c),
                          functools.partial(_slab_lanes, c), hist, hpad, nseq, seg, chained=not has_state)
        zs_ref[:, j * cc:(j + 1) * cc] = _silu(pbuf_ref[2])
        if j + 1 < nch:
            for piece in range(n_piece):
                proj_piece(0, j + 1, piece)
                proj_piece(1, j + 1, piece)
        cv = read_conv(j)
        if j == 0:
            shift = jnp.mean(cv, axis=-1, keepdims=True)
        dv = cv - shift
        a1 = jnp.sum(dv, axis=-1, keepdims=True)
        a2 = jnp.sum(dv * dv, axis=-1, keepdims=True)
        s1 = a1 if s1 is None else s1 + a1
        s2 = a2 if s2 is None else s2 + a2
    m1 = s1 * (1.0 / D_INNER)
    mu = shift + m1
    var = s2 * (1.0 / D_INNER) - m1 * m1
    rstd = lax.rsqrt(var + EPS)

    out = None
    for j in range(nch):
        cols = slice(j * cc, (j + 1) * cc)
        yn = (read_conv(j) - mu) * rstd * lng_ref[:, cols] + lnb_ref[:, cols]
        part = _dot((_silu(yn) * zs_ref[:, cols]).astype(BF16), wout_ref[cols, :])
        out = part if out is None else out + part

    _finish(x_ref, mod_ref, out, gfin_ref, o_ref, nseq, seg)


def _layer_d_kernel(nseq, seg, has_state, has_final, *refs):
    refs = list(refs)
    x_ref, mod_ref, g_ref = refs[:3]
    refs = refs[3:]
    state_ref = refs.pop(0) if has_state else None
    win_ref, wpool_ref, scale_ref, wout_ref = refs[:4]
    refs = refs[4:]
    gfin_ref = refs.pop(0) if has_final else None
    o_ref, state_out_ref, h_ref, ext_ref, dbuf_ref, y_ref = refs
    hist, hpad = POOL_HIST, _round_up(POOL_HIST, SUBLANES)
    cc = POOL_GROUP_W
    pitch = _group_pitch(seg)
    spg = cc // LANES

    _init_slab_history(ext_ref, state_ref, hist, hpad, nseq, seg)
    _prenorm(x_ref, mod_ref, g_ref, h_ref, nseq, seg)
    h = h_ref[...]

    lane_row = lax.broadcasted_iota(jnp.int32, (SUBLANES, LANES), 0) * pitch

    for gidx, win in enumerate(POOL_WINDOWS):
        cols = slice(gidx * cc, (gidx + 1) * cc)
        pv = _dot(h, win_ref[:, cols])
        z = _dot(h, win_ref[:, D_INNER + gidx * cc:D_INNER + (gidx + 1) * cc])
        for s in range(nseq):
            for ci in range(spg):
                ext_ref[s, gidx * spg + ci, hpad:hpad + seg, :] = (
                    pv[s * seg:(s + 1) * seg, ci * LANES:(ci + 1) * LANES])
        if not has_state:
            for ci in range(spg):
                _link_history(ext_ref, functools.partial(_slab_rows, gidx * spg + ci), hist, hpad, nseq, seg)
        for s in range(nseq):
            pos0 = PAST_LEN if has_state else pl.program_id(1) * (nseq * seg) + s * seg

            def rows_at(ci, r, s=s):
                return ext_ref[s, gidx * spg + ci, pl.ds(hpad + r, SUBLANES, stride=pitch), :]

            tots = [None] * spg
            for v in range(pitch):
                inv_cnt = 1.0 / jnp.minimum(win, lane_row + (pos0 + v + 1)).astype(F32)
                for ci in range(spg):
                    cur = rows_at(ci, v)
                    if v == 0:
                        tots[ci] = cur
                        for d in range(1, win):
                            tots[ci] = tots[ci] + rows_at(ci, -d)
                    else:
                        tots[ci] = tots[ci] + cur - rows_at(ci, v - win)
                    dbuf_ref[s, ci, pl.ds(v, SUBLANES, stride=pitch), :] = tots[ci] * inv_cnt - cur
        for ci in range(spg):
            c = gidx * spg + ci
            _roll_history(ext_ref, state_out_ref, functools.partial(_slab_rows, c),
                          functools.partial(_slab_lanes, c), hist, hpad, nseq, seg, chained=not has_state)
        dlt = jnp.concatenate(
            [jnp.concatenate([dbuf_ref[s, ci, 0:seg, :] for ci in range(spg)], axis=1) for s in range(nseq)],
            axis=0)
        yp = _dot(dlt.astype(BF16), wpool_ref[gidx]) * scale_ref[:, cols]
        y_ref[:, cols] = (yp * _silu(z)).astype(BF16)

    _finish(x_ref, mod_ref, _dot(y_ref[...], wout_ref[...]), gfin_ref, o_ref, nseq, seg)


def _const_spec(arr):
    nd = arr.ndim
    return pl.BlockSpec(arr.shape, lambda b, t, _nd=nd: (0,) * _nd, pipeline_mode=pl.Buffered(1))


def _run_layer(body, name, x, mod, g, state, weights, g_final, hist, hpad, nseq, seg, n_batch_steps,
               n_tiles, chained, extra_scratch, emit_v=False, with_state_out=True, ext_shape=None, stage_y=True):
    rows_t = nseq * seg
    m_rows = x.shape[0]
    seq_blk = 1 if chained else nseq
    n_seq_total = n_batch_steps * seq_blk
    in_arrays = [x, mod, g]
    in_specs = [
        pl.BlockSpec((rows_t, D_MODEL), lambda b, t: (b * n_tiles + t, 0)),
        pl.BlockSpec((seq_blk, 3, D_MODEL), lambda b, t: (b, 0, 0)),
        _const_spec(g),
    ]
    if state is not None:
        in_arrays.append(state)
        in_specs.append(_const_spec(state))
    for w in weights:
        in_arrays.append(w)
        in_specs.append(_const_spec(w))
    if g_final is not None:
        in_arrays.append(g_final)
        in_specs.append(_const_spec(g_final))

    out_shapes = [jax.ShapeDtypeStruct((m_rows, D_MODEL), F32)]
    out_specs = [pl.BlockSpec((rows_t, D_MODEL), lambda b, t: (b * n_tiles + t, 0))]
    if with_state_out:
        out_shapes.append(jax.ShapeDtypeStruct((n_seq_total, hist, D_INNER), F32))
        out_specs.append(pl.BlockSpec((seq_blk, hist, D_INNER), lambda b, t: (b, 0, 0)))
    if emit_v:
        out_shapes.append(jax.ShapeDtypeStruct((m_rows, D_INNER), F32))
        out_specs.append(pl.BlockSpec((rows_t, D_INNER), lambda b, t: (b * n_tiles + t, 0)))

    scratch = [pltpu.VMEM((rows_t, D_MODEL), BF16)]
    if with_state_out:
        scratch.append(pltpu.VMEM(ext_shape or (nseq, hpad + seg, D_INNER), F32))
    scratch += extra_scratch
    if stage_y:
        scratch.append(pltpu.VMEM((rows_t, D_INNER), BF16))

    return pl.pallas_call(
        body,
        grid=(n_batch_steps, n_tiles),
        in_specs=in_specs,
        out_specs=out_specs,
        out_shape=out_shapes,
        scratch_shapes=scratch,
        compiler_params=pltpu.CompilerParams(
            dimension_semantics=("arbitrary", "arbitrary"), vmem_limit_bytes=VMEM_LIMIT_BYTES),
        name=name,
    )(*in_arrays)


def _pack_conv_weights(w):
    taps = w.shape[0]
    pairs = D_INNER // (2 * LANES)
    w = w.reshape(taps, pairs, 2, 1, LANES).transpose(1, 0, 2, 3, 4)
    w = jnp.broadcast_to(w, (pairs, taps, 2, SUBLANES, LANES))
    return w.reshape(pairs, taps, 2 * SUBLANES, LANES)


def _mix_matrices(w_s_a, b_s_a, seg):
    blk = MLP_BLOCK if seg >= MLP_BLOCK else seg
    cidx = jnp.arange(blk) // CHUNK
    mask = cidx[:, None] >= cidx[None, :]
    w = jnp.where(mask[None], w_s_a[:, :blk, :blk], jnp.zeros((), w_s_a.dtype))
    reps = MIX_ROWS // blk
    eye = jnp.eye(reps, dtype=w.dtype)
    wmix = jnp.einsum('rs,gij->grisj', eye, w).reshape(MLP_GROUPS, reps * blk, reps * blk)
    bmix = jnp.tile(b_s_a[:, :blk], (1, reps))[:, :, None]
    return wmix.astype(BF16), bmix.astype(F32)


def _trunk(x, mod, state_b, state_c, state_d, nseq, seg, n_batch_steps, n_tiles, emit_v, prm):
    has_state = state_b is not None
    rows_t = nseq * seg
    row = lambda v: v.reshape(1, -1)
    wmix, bmix = _mix_matrices(prm["w_s_a"], prm["b_s_a"], seg)
    common = dict(nseq=nseq, seg=seg, n_batch_steps=n_batch_steps, n_tiles=n_tiles, chained=not has_state)

    res = _run_layer(
        functools.partial(_layer_a_kernel, nseq, seg, emit_v, False), "layer_a",
        x, mod[0], row(prm["g_norm"][0]), None,
        [prm["wa_in"], row(prm["ln_a_g"]), row(prm["ln_a_b"]), wmix, bmix, prm["wa_out"]],
        None, 0, 0, extra_scratch=[pltpu.VMEM((rows_t, D_INNER), F32), pltpu.VMEM((rows_t, D_INNER), F32)],
        emit_v=emit_v,
        with_state_out=False, **common)
    x = res[0]
    v_rows = res[1] if emit_v else None

    hist, hpad = SCONV_W - 1, SUBLANES
    x, conv_b = _run_layer(
        functools.partial(_layer_b_kernel, nseq, seg, has_state, False), "layer_b",
        x, mod[1], row(prm["g_norm"][1]), state_b,
        [prm["wb_in"], prm["w_conv_b"], prm["wb_out"]],
        None, hist, hpad, extra_scratch=[], **common)

    hist, hpad = CCONV_W - 1, _round_up(CCONV_W - 1, SUBLANES)
    pitch = _group_pitch(seg)
    x, conv_c = _run_layer(
        functools.partial(_layer_c_kernel, nseq, seg, has_state, False), "layer_c",
        x, mod[2], row(prm["g_norm"][2]), state_c,
        [prm["wc_in"], prm["wc_conv"], prm["bc_conv"], row(prm["ln_c_g"]),
         row(prm["ln_c_b"]), prm["wc_out"]],
        None, hist, hpad,
        extra_scratch=[pltpu.VMEM((nseq, D_INNER // LANES, SUBLANES * pitch, LANES), F32),
                       pltpu.VMEM((rows_t, D_INNER), F32),
                       pltpu.VMEM((3, rows_t, COL_CHUNK), F32)],
        ext_shape=(nseq, D_INNER // LANES, hpad + SUBLANES * pitch, LANES), stage_y=False, **common)

    hist, hpad = POOL_HIST, _round_up(POOL_HIST, SUBLANES)
    y, pool_d = _run_layer(
        functools.partial(_layer_d_kernel, nseq, seg, has_state, True), "layer_d",
        x, mod[3], row(prm["g_norm"][3]), state_d,
        [prm["wd_in"], prm["wd_pool"], row(prm["scale_pool_d"]), prm["wd_out"]],
        row(prm["g_final"]), hist, hpad,
        extra_scratch=[pltpu.VMEM((nseq, POOL_GROUP_W // LANES, SUBLANES * pitch, LANES), F32)],
        ext_shape=(nseq, D_INNER // LANES, hpad + SUBLANES * pitch, LANES), **common)
    return y, v_rows, conv_b, conv_c, pool_d


def kernel(x_prompt, x_sample, state_conv_b, state_conv_c, state_pool_d, c_prompt, c_sample, w_ada, b_ada, g_norm, w_in_a, ln_a_g, ln_a_b, w_s_a, b_s_a, w_out_a, w_in_b, w_conv_b, w_out_b, w_in_c, w_conv_c, b_conv_c, ln_c_g, ln_c_b, w_out_c, w_in_d, w_pool_d, scale_pool_d, w_out_d, g_final):
    n_p, s_p, _ = x_prompt.shape
    n_s, s_s, _ = x_sample.shape

    c_all = jnp.concatenate([c_prompt, c_sample], axis=0)
    pad = _round_up(n_p + n_s, SUBLANES) - (n_p + n_s)
    c_all = jnp.pad(c_all, ((0, pad), (0, 0)))
    mod = _modulation(c_all, w_ada, b_ada)
    mod_p = mod[:, :n_p].reshape(DEPTH, n_p, 3, D_MODEL)
    mod_s = mod[:, n_p:n_p + n_s].reshape(DEPTH, n_s, 3, D_MODEL)

    prm = dict(
        g_norm=g_norm, ln_a_g=ln_a_g, ln_a_b=ln_a_b, w_s_a=w_s_a, b_s_a=b_s_a,
        w_conv_b=w_conv_b, ln_c_g=ln_c_g, ln_c_b=ln_c_b,
        scale_pool_d=scale_pool_d, g_final=g_final,
        wa_in=w_in_a.astype(BF16),
        wa_out=w_out_a.astype(BF16),
        wb_in=w_in_b.astype(BF16),
        wb_out=w_out_b.astype(BF16),
        wc_in=w_in_c.astype(BF16),
        wc_conv=_pack_conv_weights(w_conv_c).astype(BF16),
        bc_conv=_pack_conv_weights(b_conv_c.reshape(1, -1))[:, 0],
        wc_out=w_out_c.astype(BF16),
        wd_in=w_in_d.astype(BF16),
        wd_pool=w_pool_d.astype(BF16),
        wd_out=w_out_d.astype(BF16),
    )

    y_p, _, cb_p, cc_p, pd_p = _trunk(
        x_prompt.reshape(n_p * s_p, D_MODEL), mod_p, None, None, None,
        nseq=ROW_TILE // SEG_ROWS, seg=SEG_ROWS, n_batch_steps=n_p, n_tiles=s_p // ROW_TILE, emit_v=False, prm=prm)
    y_s, v_s, cb_s, cc_s, pd_s = _trunk(
        x_sample.reshape(n_s * s_s, D_MODEL), mod_s, state_conv_b, state_conv_c, state_pool_d,
        nseq=n_s, seg=s_s, n_batch_steps=1, n_tiles=1, emit_v=True, prm=prm)

    return (y_p.reshape(n_p, s_p, D_MODEL), y_s.reshape(n_s, s_s, D_MODEL),
            v_s.reshape(n_s, s_s, D_INNER), cb_p, cb_s, cc_p, cc_s, pd_p, pd_s)
```

```python
import functools

import jax
import jax.numpy as jnp
from jax import lax
from jax.experimental import pallas as pl
from jax.experimental.pallas import tpu as pltpu

D_MODEL = 1024
D_INNER = 2048
DEPTH = 4
CHUNK = 64
MLP_BLOCK = 128
MLP_GROUPS = 8
MLP_GROUP_W = D_INNER // MLP_GROUPS
SCONV_W = 3
CCONV_W = 31
POOL_WINDOWS = (2, 4, 8, 16)
POOL_GROUP_W = D_INNER // len(POOL_WINDOWS)
POOL_HIST = max(POOL_WINDOWS) - 1
PAST_LEN = 2048
EPS = 1e-6

F32 = jnp.float32
BF16 = jnp.bfloat16

SUBLANES = 8
LANES = 128
VACC_UNITS = 4
CONV_BLOCK = 8
VMEM_LIMIT_BYTES = 58 * 1024 * 1024

ROW_TILE = 512
SEG_ROWS = 256
MIX_ROWS = 256
COL_CHUNK = 512
INV_SQRT2 = 0.7071067811865476


def _round_up(n, m):
    return (n + m - 1) // m * m


def _gelu(x):
    return 0.5 * x * (1.0 + lax.erf(x * INV_SQRT2))


def _silu(x):
    return x * jax.nn.sigmoid(x)


def _dot(a, b):
    return jnp.dot(a, b, preferred_element_type=F32)


def _mod_kernel(c_ref, w_ref, b_ref, o_ref):
    c = c_ref[...]
    a = _silu(c).astype(BF16)
    o_ref[0] = _dot(a, w_ref[0].astype(BF16)) + b_ref[0]


def _modulation(c_all, w_ada, b_ada):
    rows = c_all.shape[0]
    nb = 1024
    return pl.pallas_call(
        _mod_kernel,
        grid=(DEPTH, 3 * D_MODEL // nb),
        in_specs=[
            pl.BlockSpec((rows, D_MODEL), lambda i, j: (0, 0)),
            pl.BlockSpec((1, D_MODEL, nb), lambda i, j: (i, 0, j)),
            pl.BlockSpec((1, 1, nb), lambda i, j: (i, 0, j)),
        ],
        out_specs=pl.BlockSpec((1, rows, nb), lambda i, j: (i, 0, j)),
        out_shape=jax.ShapeDtypeStruct((DEPTH, rows, 3 * D_MODEL), F32),
        compiler_params=pltpu.CompilerParams(
            dimension_semantics=("arbitrary", "arbitrary"), vmem_limit_bytes=VMEM_LIMIT_BYTES),
        name="ada_modulation",
    )(c_all, w_ada, b_ada.reshape(DEPTH, 1, 3 * D_MODEL))


def _prenorm(x_ref, mod_ref, g_ref, h_ref, nseq, seg):
    g = g_ref[...]
    for s in range(nseq):
        rows = slice(s * seg, (s + 1) * seg)
        x = x_ref[rows, :]
        ms = jnp.mean(x * x, axis=-1, keepdims=True)
        y = (x * lax.rsqrt(ms + EPS)) * g
        m = s if mod_ref.shape[0] > 1 else 0
        h = y * (1.0 + mod_ref[m, 1:2, :]) + mod_ref[m, 0:1, :]
        h_ref[rows, :] = h.astype(BF16)


def _zero_of(value):
    word = lax.bitcast_convert_type(value, jnp.uint32)
    return lax.shift_right_logical(lax.shift_right_logical(word, jnp.uint32(16)), jnp.uint32(16))


def _finish(x_ref, mod_ref, out, gfin_ref, o_ref, nseq, seg):
    for s in range(nseq):
        rows = slice(s * seg, (s + 1) * seg)
        m = s if mod_ref.shape[0] > 1 else 0
        xn = x_ref[rows, :] + mod_ref[m, 2:3, :] * out[rows, :]
        if gfin_ref is not None:
            ms = jnp.mean(xn * xn, axis=-1, keepdims=True)
            xn = (xn * lax.rsqrt(ms + EPS)) * gfin_ref[...]
        o_ref[rows, :] = xn


def _init_history(ext_ref, state_ref, hist, hpad, nseq):
    if state_ref is None:
        @pl.when(pl.program_id(1) == 0)
        def _():
            ext_ref[:, 0:hpad, :] = jnp.zeros((nseq, hpad, D_INNER), F32)
    else:
        for s in range(nseq):
            ext_ref[s, hpad - hist:hpad, :] = state_ref[s]


def _init_slab_history(ext_ref, state_ref, hist, hpad, nseq, seg):
    _, n_slab, ext_rows, _ = ext_ref.shape
    tail = ext_rows - hpad - seg
    if state_ref is None:
        @pl.when(pl.program_id(1) == 0)
        def _():
            ext_ref[:, :, 0:hpad, :] = jnp.zeros((nseq, n_slab, hpad, LANES), F32)
            if tail:
                ext_ref[:, :, hpad + seg:ext_rows, :] = jnp.zeros((nseq, n_slab, tail, LANES), F32)
    else:
        for s in range(nseq):
            for c in range(n_slab):
                ext_ref[s, c, hpad - hist:hpad, :] = state_ref[s, :, c * LANES:(c + 1) * LANES]
        if tail:
            ext_ref[:, :, hpad + seg:ext_rows, :] = jnp.zeros((nseq, n_slab, tail, LANES), F32)


def _slab_rows(c, s, r0, r1):
    return (s, c, slice(r0, r1), slice(None))


def _slab_lanes(c, s):
    return (s, slice(None), slice(c * LANES, (c + 1) * LANES))


def _group_pitch(seg):
    pitch = seg // SUBLANES
    while pitch % 8 == 0:
        pitch += 1
    return pitch


def _link_history(ext_ref, at, hist, hpad, nseq, seg):
    for s in range(1, nseq):
        ext_ref[at(s, hpad - hist, hpad)] = ext_ref[at(s - 1, hpad + seg - hist, hpad + seg)]


def _roll_history(ext_ref, state_out_ref, at, out_at, hist, hpad, nseq, seg, chained):
    if chained:
        last = ext_ref[at(nseq - 1, hpad + seg - hist, hpad + seg)]
        state_out_ref[out_at(0)] = last
        ext_ref[at(0, hpad - hist, hpad)] = last
    else:
        for s in range(nseq):
            last = ext_ref[at(s, hpad + seg - hist, hpad + seg)]
            state_out_ref[out_at(s)] = last
            ext_ref[at(s, hpad - hist, hpad)] = last


def _layer_a_kernel(nseq, seg, emit_v, has_final, *refs):
    refs = list(refs)
    x_ref, mod_ref, g_ref, win_ref, lng_ref, lnb_ref, wmix_ref, bmix_ref, wout_ref = refs[:9]
    refs = refs[9:]
    gfin_ref = refs.pop(0) if has_final else None
    o_ref = refs.pop(0)
    v_out_ref = refs.pop(0) if emit_v else None
    h_ref, vbuf_ref, y_ref = refs
    rows_t = nseq * seg
    mix_rows = wmix_ref.shape[1]
    nch = D_INNER // COL_CHUNK

    _prenorm(x_ref, mod_ref, g_ref, h_ref, nseq, seg)
    h = h_ref[...]

    shift = s1 = s2 = None
    for j in range(nch):
        cols = slice(j * COL_CHUNK, (j + 1) * COL_CHUNK)
        gv = _gelu(_dot(h, win_ref[:, D_INNER + j * COL_CHUNK:D_INNER + (j + 1) * COL_CHUNK]))
        vbuf_ref[:, cols] = gv
        if j == 0:
            shift = jnp.mean(gv, axis=-1, keepdims=True)
        dv = gv - shift
        a1 = jnp.sum(dv, axis=-1, keepdims=True)
        a2 = jnp.sum(dv * dv, axis=-1, keepdims=True)
        s1 = a1 if s1 is None else s1 + a1
        s2 = a2 if s2 is None else s2 + a2
    m1 = s1 * (1.0 / D_INNER)
    mu = shift + m1
    var = s2 * (1.0 / D_INNER) - m1 * m1
    rstd = lax.rsqrt(var + EPS)

    gpc = COL_CHUNK // MLP_GROUP_W
    for j in range(nch):
        cols = slice(j * COL_CHUNK, (j + 1) * COL_CHUNK)
        vn = (vbuf_ref[:, cols] - mu) * rstd * lng_ref[:, cols] + lnb_ref[:, cols]
        if emit_v:
            v_out_ref[:, cols] = vn
        vnb = vn.astype(BF16)
        mixed = []
        for gi in range(gpc):
            g = j * gpc + gi
            gcols = slice(gi * MLP_GROUP_W, (gi + 1) * MLP_GROUP_W)
            mixed.append(jnp.concatenate(
                [_dot(wmix_ref[g], vnb[r0:r0 + mix_rows, gcols]) + bmix_ref[g]
                 for r0 in range(0, rows_t, mix_rows)], axis=0))
        mixed = jnp.concatenate(mixed, axis=1)
        u = _gelu(_dot(h, win_ref[:, cols]))
        z = _dot(h, win_ref[:, 2 * D_INNER + j * COL_CHUNK:2 * D_INNER + (j + 1) * COL_CHUNK])
        y_ref[:, cols] = (u * mixed * _silu(z)).astype(BF16)

    _finish(x_ref, mod_ref, _dot(y_ref[...], wout_ref[...]), gfin_ref, o_ref, nseq, seg)


def _layer_b_kernel(nseq, seg, has_state, has_final, *refs):
    refs = list(refs)
    x_ref, mod_ref, g_ref = refs[:3]
    refs = refs[3:]
    state_ref = refs.pop(0) if has_state else None
    win_ref, wconv_ref, wout_ref = refs[:3]
    refs = refs[3:]
    gfin_ref = refs.pop(0) if has_final else None
    o_ref, state_out_ref, h_ref, ext_ref, y_ref = refs
    hist, hpad = SCONV_W - 1, SUBLANES
    nch = D_INNER // COL_CHUNK
    cc = COL_CHUNK

    _prenorm(x_ref, mod_ref, g_ref, h_ref, nseq, seg)
    _init_history(ext_ref, state_ref, hist, hpad, nseq)
    h = h_ref[...]

    for j in range(nch):
        cols = slice(j * cc, (j + 1) * cc)
        bg, cg, hv, z = [
            _dot(h, win_ref[:, br * D_INNER + j * cc:br * D_INNER + (j + 1) * cc]) for br in range(4)]
        p = cg * hv
        gate = bg * _silu(z)
        at = lambda s, r0, r1, cols=cols: (s, slice(r0, r1), cols)
        for s in range(nseq):
            ext_ref[s, hpad:hpad + seg, cols] = p[s * seg:(s + 1) * seg, :]
        if not has_state:
            _link_history(ext_ref, at, hist, hpad, nseq, seg)
        for s in range(nseq):
            rows = slice(s * seg, (s + 1) * seg)
            conv = p[rows, :] * wconv_ref[2:3, cols]
            for k in range(SCONV_W - 1):
                d = SCONV_W - 1 - k
                conv = conv + ext_ref[s, hpad - d:hpad - d + seg, cols] * wconv_ref[k:k + 1, cols]
            y_ref[rows, cols] = (conv * gate[rows, :]).astype(BF16)
        _roll_history(ext_ref, state_out_ref, at, lambda s, cols=cols: (s, slice(None), cols),
                      hist, hpad, nseq, seg, chained=not has_state)

    _finish(x_ref, mod_ref, _dot(y_ref[...], wout_ref[...]), gfin_ref, o_ref, nseq, seg)


def _layer_c_kernel(nseq, seg, has_state, has_final, *refs):
    refs = list(refs)
    x_ref, mod_ref, g_ref = refs[:3]
    refs = refs[3:]
    state_ref = refs.pop(0) if has_state else None
    win_ref, wconv_ref, bconv_ref, lng_ref, lnb_ref, wout_ref = refs[:6]
    refs = refs[6:]
    gfin_ref = refs.pop(0) if has_final else None
    o_ref, state_out_ref, h_ref, ext_ref, cbuf_ref, zs_ref = refs
    hist, hpad = CCONV_W - 1, _round_up(CCONV_W - 1, SUBLANES)
    nch = D_INNER // COL_CHUNK
    cc = COL_CHUNK
    pitch = _group_pitch(seg)
    pairs_per_chunk = cc // LANES // 2

    _init_slab_history(ext_ref, state_ref, hist, hpad, nseq, seg)
    _prenorm(x_ref, mod_ref, g_ref, h_ref, nseq, seg)

    h = h_ref[...]

    def in_proj(branch, j):
        lo = branch * D_INNER + j * cc
        return _dot(h, win_ref[:, lo:lo + cc])

    def conv_unit(j, s, unit, carry):
        p = j * pairs_per_chunk + unit
        c0 = 2 * p
        bias = bconv_ref[p]
        prev = None
        srcs = {}
        for v0 in range(0, pitch, VACC_UNITS):
            n_acc = min(VACC_UNITS, pitch - v0)
            if v0 % CONV_BLOCK == 0:
                srcs = {}
                for u in range(v0, min(v0 + CONV_BLOCK, pitch) + CCONV_W - 1):
                    start = hpad - (CCONV_W - 1) + u
                    srcs[u] = jnp.concatenate(
                        [ext_ref[s, c0, pl.ds(start, SUBLANES, stride=pitch), :],
                         ext_ref[s, c0 + 1, pl.ds(start, SUBLANES, stride=pitch), :]], axis=0).astype(BF16)
            accs = [None] * n_acc
            for k in range(CCONV_W):
                wk = wconv_ref[p, k]
                if k == 0 and prev is not None:
                    wk = pltpu.bitcast(pltpu.bitcast(wk, jnp.uint32) | _zero_of(prev), BF16)
                wk = wk.astype(F32)
                for i in range(n_acc):
                    term = srcs[v0 + i + k].astype(F32) * wk
                    accs[i] = term if accs[i] is None else accs[i] + term
            prev = accs[n_acc - 1][0:SUBLANES]
            for i in range(n_acc):
                res = accs[i] + bias
                cbuf_ref[s, c0, pl.ds(v0 + i, SUBLANES, stride=pitch), :] = res[0:SUBLANES]
                cbuf_ref[s, c0 + 1, pl.ds(v0 + i, SUBLANES, stride=pitch), :] = res[SUBLANES:2 * SUBLANES]
        return carry

    def read_conv(j):
        return jnp.concatenate(
            [jnp.concatenate([cbuf_ref[s, j * (cc // LANES) + ci, 0:seg, :] for ci in range(cc // LANES)], axis=1)
             for s in range(nseq)], axis=0)

    shift = s1 = s2 = None
    for j in range(nch):
        glu = in_proj(0, j) * jax.nn.sigmoid(in_proj(1, j))
        for s in range(nseq):
            for ci in range(cc // LANES):
                ext_ref[s, j * (cc // LANES) + ci, hpad:hpad + seg, :] = (
                    glu[s * seg:(s + 1) * seg, ci * LANES:(ci + 1) * LANES])
        if not has_state:
            for ci in range(cc // LANES):
                _link_history(ext_ref, functools.partial(_slab_rows, j * (cc // LANES) + ci), hist, hpad, nseq, seg)
        zs_ref[:, j * cc:(j + 1) * cc] = _silu(in_proj(2, j))
        for s in range(nseq):
            lax.fori_loop(0, pairs_per_chunk, functools.partial(conv_unit, j, s), 0)
        for ci in range(cc // LANES):
            c = j * (cc // LANES) + ci
            _roll_history(ext_ref, state_out_ref, functools.partial(_slab_rows, c),
                          functools.partial(_slab_lanes, c), hist, hpad, nseq, seg, chained=not has_state)
        cv = read_conv(j)
        if j == 0:
            shift = jnp.mean(cv, axis=-1, keepdims=True)
        dv = cv - shift
        a1 = jnp.sum(dv, axis=-1, keepdims=True)
        a2 = jnp.sum(dv * dv, axis=-1, keepdims=True)
        s1 = a1 if s1 is None else s1 + a1
        s2 = a2 if s2 is None else s2 + a2
    m1 = s1 * (1.0 / D_INNER)
    mu = shift + m1
    var = s2 * (1.0 / D_INNER) - m1 * m1
    rstd = lax.rsqrt(var + EPS)

    out = None
    for j in range(nch):
        cols = slice(j * cc, (j + 1) * cc)
        yn = (read_conv(j) - mu) * rstd * lng_ref[:, cols] + lnb_ref[:, cols]
        part = _dot((_silu(yn) * zs_ref[:, cols]).astype(BF16), wout_ref[cols, :])
        out = part if out is None else out + part

    _finish(x_ref, mod_ref, out, gfin_ref, o_ref, nseq, seg)


def _layer_d_kernel(nseq, seg, has_state, has_final, *refs):
    refs = list(refs)
    x_ref, mod_ref, g_ref = refs[:3]
    refs = refs[3:]
    state_ref = refs.pop(0) if has_state else None
    win_ref, wpool_ref, scale_ref, wout_ref = refs[:4]
    refs = refs[4:]
    gfin_ref = refs.pop(0) if has_final else None
    o_ref, state_out_ref, h_ref, ext_ref, dbuf_ref, y_ref = refs
    hist, hpad = POOL_HIST, _round_up(POOL_HIST, SUBLANES)
    cc = POOL_GROUP_W
    pitch = _group_pitch(seg)
    spg = cc // LANES

    _init_slab_history(ext_ref, state_ref, hist, hpad, nseq, seg)
    _prenorm(x_ref, mod_ref, g_ref, h_ref, nseq, seg)
    h = h_ref[...]

    lane_row = lax.broadcasted_iota(jnp.int32, (SUBLANES, LANES), 0) * pitch

    for gidx, win in enumerate(POOL_WINDOWS):
        cols = slice(gidx * cc, (gidx + 1) * cc)
        pv = _dot(h, win_ref[:, cols])
        z = _dot(h, win_ref[:, D_INNER + gidx * cc:D_INNER + (gidx + 1) * cc])
        for s in range(nseq):
            for ci in range(spg):
                ext_ref[s, gidx * spg + ci, hpad:hpad + seg, :] = (
                    pv[s * seg:(s + 1) * seg, ci * LANES:(ci + 1) * LANES])
        if not has_state:
            for ci in range(spg):
                _link_history(ext_ref, functools.partial(_slab_rows, gidx * spg + ci), hist, hpad, nseq, seg)
        for s in range(nseq):
            pos0 = PAST_LEN if has_state else pl.program_id(1) * (nseq * seg) + s * seg

            def rows_at(ci, r, s=s):
                return ext_ref[s, gidx * spg + ci, pl.ds(hpad + r, SUBLANES, stride=pitch), :]

            tots = [None] * spg
            for v in range(pitch):
                inv_cnt = 1.0 / jnp.minimum(win, lane_row + (pos0 + v + 1)).astype(F32)
                for ci in range(spg):
                    cur = rows_at(ci, v)
                    if v == 0:
                        tots[ci] = cur
                        for d in range(1, win):
                            tots[ci] = tots[ci] + rows_at(ci, -d)
                    else:
                        tots[ci] = tots[ci] + cur - rows_at(ci, v - win)
                    dbuf_ref[s, ci, pl.ds(v, SUBLANES, stride=pitch), :] = tots[ci] * inv_cnt - cur
        for ci in range(spg):
            c = gidx * spg + ci
            _roll_history(ext_ref, state_out_ref, functools.partial(_slab_rows, c),
                          functools.partial(_slab_lanes, c), hist, hpad, nseq, seg, chained=not has_state)
        dlt = jnp.concatenate(
            [jnp.concatenate([dbuf_ref[s, ci, 0:seg, :] for ci in range(spg)], axis=1) for s in range(nseq)],
            axis=0)
        yp = _dot(dlt.astype(BF16), wpool_ref[gidx]) * scale_ref[:, cols]
        y_ref[:, cols] = (yp * _silu(z)).astype(BF16)

    _finish(x_ref, mod_ref, _dot(y_ref[...], wout_ref[...]), gfin_ref, o_ref, nseq, seg)


def _const_spec(arr):
    nd = arr.ndim
    return pl.BlockSpec(arr.shape, lambda b, t, _nd=nd: (0,) * _nd, pipeline_mode=pl.Buffered(1))


def _run_layer(body, name, x, mod, g, state, weights, g_final, hist, hpad, nseq, seg, n_batch_steps,
               n_tiles, chained, extra_scratch, emit_v=False, with_state_out=True, ext_shape=None, stage_y=True):
    rows_t = nseq * seg
    m_rows = x.shape[0]
    seq_blk = 1 if chained else nseq
    n_seq_total = n_batch_steps * seq_blk
    in_arrays = [x, mod, g]
    in_specs = [
        pl.BlockSpec((rows_t, D_MODEL), lambda b, t: (b * n_tiles + t, 0)),
        pl.BlockSpec((seq_blk, 3, D_MODEL), lambda b, t: (b, 0, 0)),
        _const_spec(g),
    ]
    if state is not None:
        in_arrays.append(state)
        in_specs.append(_const_spec(state))
    for w in weights:
        in_arrays.append(w)
        in_specs.append(_const_spec(w))
    if g_final is not None:
        in_arrays.append(g_final)
        in_specs.append(_const_spec(g_final))

    out_shapes = [jax.ShapeDtypeStruct((m_rows, D_MODEL), F32)]
    out_specs = [pl.BlockSpec((rows_t, D_MODEL), lambda b, t: (b * n_tiles + t, 0))]
    if with_state_out:
        out_shapes.append(jax.ShapeDtypeStruct((n_seq_total, hist, D_INNER), F32))
        out_specs.append(pl.BlockSpec((seq_blk, hist, D_INNER), lambda b, t: (b, 0, 0)))
    if emit_v:
        out_shapes.append(jax.ShapeDtypeStruct((m_rows, D_INNER), F32))
        out_specs.append(pl.BlockSpec((rows_t, D_INNER), lambda b, t: (b * n_tiles + t, 0)))

    scratch = [pltpu.VMEM((rows_t, D_MODEL), BF16)]
    if with_state_out:
        scratch.append(pltpu.VMEM(ext_shape or (nseq, hpad + seg, D_INNER), F32))
    scratch += extra_scratch
    if stage_y:
        scratch.append(pltpu.VMEM((rows_t, D_INNER), BF16))

    return pl.pallas_call(
        body,
        grid=(n_batch_steps, n_tiles),
        in_specs=in_specs,
        out_specs=out_specs,
        out_shape=out_shapes,
        scratch_shapes=scratch,
        compiler_params=pltpu.CompilerParams(
            dimension_semantics=("arbitrary", "arbitrary"), vmem_limit_bytes=VMEM_LIMIT_BYTES),
        name=name,
    )(*in_arrays)


def _pack_conv_weights(w):
    taps = w.shape[0]
    pairs = D_INNER // (2 * LANES)
    w = w.reshape(taps, pairs, 2, 1, LANES).transpose(1, 0, 2, 3, 4)
    w = jnp.broadcast_to(w, (pairs, taps, 2, SUBLANES, LANES))
    return w.reshape(pairs, taps, 2 * SUBLANES, LANES)


def _mix_matrices(w_s_a, b_s_a, seg):
    blk = MLP_BLOCK if seg >= MLP_BLOCK else seg
    cidx = jnp.arange(blk) // CHUNK
    mask = cidx[:, None] >= cidx[None, :]
    w = jnp.where(mask[None], w_s_a[:, :blk, :blk], jnp.zeros((), w_s_a.dtype))
    reps = MIX_ROWS // blk
    eye = jnp.eye(reps, dtype=w.dtype)
    wmix = jnp.einsum('rs,gij->grisj', eye, w).reshape(MLP_GROUPS, reps * blk, reps * blk)
    bmix = jnp.tile(b_s_a[:, :blk], (1, reps))[:, :, None]
    return wmix.astype(BF16), bmix.astype(F32)


def _trunk(x, mod, state_b, state_c, state_d, nseq, seg, n_batch_steps, n_tiles, emit_v, prm):
    has_state = state_b is not None
    rows_t = nseq * seg
    row = lambda v: v.reshape(1, -1)
    wmix, bmix = _mix_matrices(prm["w_s_a"], prm["b_s_a"], seg)
    common = dict(nseq=nseq, seg=seg, n_batch_steps=n_batch_steps, n_tiles=n_tiles, chained=not has_state)

    res = _run_layer(
        functools.partial(_layer_a_kernel, nseq, seg, emit_v, False), "layer_a",
        x, mod[0], row(prm["g_norm"][0]), None,
        [prm["wa_in"], row(prm["ln_a_g"]), row(prm["ln_a_b"]), wmix, bmix, prm["wa_out"]],
        None, 0, 0, extra_scratch=[pltpu.VMEM((rows_t, D_INNER), F32)], emit_v=emit_v,
        with_state_out=False, **common)
    x = res[0]
    v_rows = res[1] if emit_v else None

    hist, hpad = SCONV_W - 1, SUBLANES
    x, conv_b = _run_layer(
        functools.partial(_layer_b_kernel, nseq, seg, has_state, False), "layer_b",
        x, mod[1], row(prm["g_norm"][1]), state_b,
        [prm["wb_in"], prm["w_conv_b"], prm["wb_out"]],
        None, hist, hpad, extra_scratch=[], **common)

    hist, hpad = CCONV_W - 1, _round_up(CCONV_W - 1, SUBLANES)
    pitch = _group_pitch(seg)
    x, conv_c = _run_layer(
        functools.partial(_layer_c_kernel, nseq, seg, has_state, False), "layer_c",
        x, mod[2], row(prm["g_norm"][2]), state_c,
        [prm["wc_in"], prm["wc_conv"], prm["bc_conv"], row(prm["ln_c_g"]),
         row(prm["ln_c_b"]), prm["wc_out"]],
        None, hist, hpad,
        extra_scratch=[pltpu.VMEM((nseq, D_INNER // LANES, SUBLANES * pitch, LANES), F32),
                       pltpu.VMEM((rows_t, D_INNER), F32)],
        ext_shape=(nseq, D_INNER // LANES, hpad + SUBLANES * pitch, LANES), stage_y=False, **common)

    hist, hpad = POOL_HIST, _round_up(POOL_HIST, SUBLANES)
    y, pool_d = _run_layer(
        functools.partial(_layer_d_kernel, nseq, seg, has_state, True), "layer_d",
        x, mod[3], row(prm["g_norm"][3]), state_d,
        [prm["wd_in"], prm["wd_pool"], row(prm["scale_pool_d"]), prm["wd_out"]],
        row(prm["g_final"]), hist, hpad,
        extra_scratch=[pltpu.VMEM((nseq, POOL_GROUP_W // LANES, SUBLANES * pitch, LANES), F32)],
        ext_shape=(nseq, D_INNER // LANES, hpad + SUBLANES * pitch, LANES), **common)
    return y, v_rows, conv_b, conv_c, pool_d


def kernel(x_prompt, x_sample, state_conv_b, state_conv_c, state_pool_d, c_prompt, c_sample, w_ada, b_ada, g_norm, w_in_a, ln_a_g, ln_a_b, w_s_a, b_s_a, w_out_a, w_in_b, w_conv_b, w_out_b, w_in_c, w_conv_c, b_conv_c, ln_c_g, ln_c_b, w_out_c, w_in_d, w_pool_d, scale_pool_d, w_out_d, g_final):
    n_p, s_p, _ = x_prompt.shape
    n_s, s_s, _ = x_sample.shape

    c_all = jnp.concatenate([c_prompt, c_sample], axis=0)
    pad = _round_up(n_p + n_s, SUBLANES) - (n_p + n_s)
    c_all = jnp.pad(c_all, ((0, pad), (0, 0)))
    mod = _modulation(c_all, w_ada, b_ada)
    mod_p = mod[:, :n_p].reshape(DEPTH, n_p, 3, D_MODEL)
    mod_s = mod[:, n_p:n_p + n_s].reshape(DEPTH, n_s, 3, D_MODEL)

    prm = dict(
        g_norm=g_norm, ln_a_g=ln_a_g, ln_a_b=ln_a_b, w_s_a=w_s_a, b_s_a=b_s_a,
        w_conv_b=w_conv_b, ln_c_g=ln_c_g, ln_c_b=ln_c_b,
        scale_pool_d=scale_pool_d, g_final=g_final,
        wa_in=w_in_a.astype(BF16),
        wa_out=w_out_a.astype(BF16),
        wb_in=w_in_b.astype(BF16),
        wb_out=w_out_b.astype(BF16),
        wc_in=w_in_c.astype(BF16),
        wc_conv=_pack_conv_weights(w_conv_c).astype(BF16),
        bc_conv=_pack_conv_weights(b_conv_c.reshape(1, -1))[:, 0],
        wc_out=w_out_c.astype(BF16),
        wd_in=w_in_d.astype(BF16),
        wd_pool=w_pool_d.astype(BF16),
        wd_out=w_out_d.astype(BF16),
    )

    y_p, _, cb_p, cc_p, pd_p = _trunk(
        x_prompt.reshape(n_p * s_p, D_MODEL), mod_p, None, None, None,
        nseq=ROW_TILE // SEG_ROWS, seg=SEG_ROWS, n_batch_steps=n_p, n_tiles=s_p // ROW_TILE, emit_v=False, prm=prm)
    y_s, v_s, cb_s, cc_s, pd_s = _trunk(
        x_sample.reshape(n_s * s_s, D_MODEL), mod_s, state_conv_b, state_conv_c, state_pool_d,
        nseq=n_s, seg=s_s, n_batch_steps=1, n_tiles=1, emit_v=True, prm=prm)

    return (y_p.reshape(n_p, s_p, D_MODEL), y_s.reshape(n_s, s_s, D_MODEL),
            v_s.reshape(n_s, s_s, D_INNER), cb_p, cb_s, cc_p, cc_s, pd_p, pd_s)
```

```python
import functools

import jax
import jax.numpy as jnp
from jax import lax
from jax.experimental import pallas as pl
from jax.experimental.pallas import tpu as pltpu

D_MODEL = 1024
D_INNER = 2048
DEPTH = 4
CHUNK = 64
MLP_BLOCK = 128
MLP_GROUPS = 8
MLP_GROUP_W = D_INNER // MLP_GROUPS
SCONV_W = 3
CCONV_W = 31
POOL_WINDOWS = (2, 4, 8, 16)
POOL_GROUP_W = D_INNER // len(POOL_WINDOWS)
POOL_HIST = max(POOL_WINDOWS) - 1
PAST_LEN = 2048
EPS = 1e-6

F32 = jnp.float32
BF16 = jnp.bfloat16

SUBLANES = 8
LANES = 128
MXU_COLS = 256
VACC_UNITS = 4
CONV_BLOCK = 8
VMEM_LIMIT_BYTES = 58 * 1024 * 1024

ROW_TILE = 512
SEG_ROWS = 256
MIX_ROWS = 256
COL_CHUNK = 512
INV_SQRT2 = 0.7071067811865476


def _round_up(n, m):
    return (n + m - 1) // m * m


def _gelu(x):
    return 0.5 * x * (1.0 + lax.erf(x * INV_SQRT2))


def _silu(x):
    return x * jax.nn.sigmoid(x)


def _dot(a, b):
    return jnp.dot(a, b, preferred_element_type=F32)


def _mod_kernel(c_ref, w_ref, b_ref, o_ref):
    c = c_ref[...]
    a = _silu(c).astype(BF16)
    o_ref[0] = _dot(a, w_ref[0].astype(BF16)) + b_ref[0]


def _modulation(c_all, w_ada, b_ada):
    rows = c_all.shape[0]
    nb = 1024
    return pl.pallas_call(
        _mod_kernel,
        grid=(DEPTH, 3 * D_MODEL // nb),
        in_specs=[
            pl.BlockSpec((rows, D_MODEL), lambda i, j: (0, 0)),
            pl.BlockSpec((1, D_MODEL, nb), lambda i, j: (i, 0, j)),
            pl.BlockSpec((1, 1, nb), lambda i, j: (i, 0, j)),
        ],
        out_specs=pl.BlockSpec((1, rows, nb), lambda i, j: (i, 0, j)),
        out_shape=jax.ShapeDtypeStruct((DEPTH, rows, 3 * D_MODEL), F32),
        compiler_params=pltpu.CompilerParams(
            dimension_semantics=("arbitrary", "arbitrary"), vmem_limit_bytes=VMEM_LIMIT_BYTES),
        name="ada_modulation",
    )(c_all, w_ada, b_ada.reshape(DEPTH, 1, 3 * D_MODEL))


def _prenorm(x_ref, mod_ref, g_ref, h_ref, nseq, seg):
    g = g_ref[...]
    for s in range(nseq):
        rows = slice(s * seg, (s + 1) * seg)
        x = x_ref[rows, :]
        ms = jnp.mean(x * x, axis=-1, keepdims=True)
        y = (x * lax.rsqrt(ms + EPS)) * g
        m = s if mod_ref.shape[0] > 1 else 0
        h = y * (1.0 + mod_ref[m, 1:2, :]) + mod_ref[m, 0:1, :]
        h_ref[rows, :] = h.astype(BF16)


def _zero_of(value):
    word = lax.bitcast_convert_type(value, jnp.uint32)
    return lax.shift_right_logical(lax.shift_right_logical(word, jnp.uint32(16)), jnp.uint32(16))


def _finish(x_ref, mod_ref, out, gfin_ref, o_ref, nseq, seg):
    for s in range(nseq):
        rows = slice(s * seg, (s + 1) * seg)
        m = s if mod_ref.shape[0] > 1 else 0
        xn = x_ref[rows, :] + mod_ref[m, 2:3, :] * out[rows, :]
        if gfin_ref is not None:
            ms = jnp.mean(xn * xn, axis=-1, keepdims=True)
            xn = (xn * lax.rsqrt(ms + EPS)) * gfin_ref[...]
        o_ref[rows, :] = xn


def _init_history(ext_ref, state_ref, hist, hpad, nseq):
    if state_ref is None:
        @pl.when(pl.program_id(1) == 0)
        def _():
            ext_ref[:, 0:hpad, :] = jnp.zeros((nseq, hpad, D_INNER), F32)
    else:
        for s in range(nseq):
            ext_ref[s, hpad - hist:hpad, :] = state_ref[s]


def _init_slab_history(ext_ref, state_ref, hist, hpad, nseq, seg):
    _, n_slab, ext_rows, _ = ext_ref.shape
    tail = ext_rows - hpad - seg
    if state_ref is None:
        @pl.when(pl.program_id(1) == 0)
        def _():
            ext_ref[:, :, 0:hpad, :] = jnp.zeros((nseq, n_slab, hpad, LANES), F32)
            if tail:
                ext_ref[:, :, hpad + seg:ext_rows, :] = jnp.zeros((nseq, n_slab, tail, LANES), F32)
    else:
        for s in range(nseq):
            for c in range(n_slab):
                ext_ref[s, c, hpad - hist:hpad, :] = state_ref[s, :, c * LANES:(c + 1) * LANES]
        if tail:
            ext_ref[:, :, hpad + seg:ext_rows, :] = jnp.zeros((nseq, n_slab, tail, LANES), F32)


def _slab_rows(c, s, r0, r1):
    return (s, c, slice(r0, r1), slice(None))


def _slab_lanes(c, s):
    return (s, slice(None), slice(c * LANES, (c + 1) * LANES))


def _group_pitch(seg):
    pitch = seg // SUBLANES
    while pitch % 8 == 0:
        pitch += 1
    return pitch


def _link_history(ext_ref, at, hist, hpad, nseq, seg):
    for s in range(1, nseq):
        ext_ref[at(s, hpad - hist, hpad)] = ext_ref[at(s - 1, hpad + seg - hist, hpad + seg)]


def _roll_history(ext_ref, state_out_ref, at, out_at, hist, hpad, nseq, seg, chained):
    if chained:
        last = ext_ref[at(nseq - 1, hpad + seg - hist, hpad + seg)]
        state_out_ref[out_at(0)] = last
        ext_ref[at(0, hpad - hist, hpad)] = last
    else:
        for s in range(nseq):
            last = ext_ref[at(s, hpad + seg - hist, hpad + seg)]
            state_out_ref[out_at(s)] = last
            ext_ref[at(s, hpad - hist, hpad)] = last


def _layer_a_kernel(nseq, seg, emit_v, has_final, *refs):
    refs = list(refs)
    x_ref, mod_ref, g_ref, win_ref, lng_ref, lnb_ref, wmix_ref, bmix_ref, wout_ref = refs[:9]
    refs = refs[9:]
    gfin_ref = refs.pop(0) if has_final else None
    o_ref = refs.pop(0)
    v_out_ref = refs.pop(0) if emit_v else None
    h_ref, vbuf_ref, y_ref = refs
    rows_t = nseq * seg
    mix_rows = wmix_ref.shape[1]
    nch = D_INNER // COL_CHUNK

    _prenorm(x_ref, mod_ref, g_ref, h_ref, nseq, seg)
    h = h_ref[...]

    shift = s1 = s2 = None
    for j in range(nch):
        cols = slice(j * COL_CHUNK, (j + 1) * COL_CHUNK)
        gv = _gelu(_dot(h, win_ref[:, D_INNER + j * COL_CHUNK:D_INNER + (j + 1) * COL_CHUNK]))
        vbuf_ref[:, cols] = gv
        if j == 0:
            shift = jnp.mean(gv, axis=-1, keepdims=True)
        dv = gv - shift
        a1 = jnp.sum(dv, axis=-1, keepdims=True)
        a2 = jnp.sum(dv * dv, axis=-1, keepdims=True)
        s1 = a1 if s1 is None else s1 + a1
        s2 = a2 if s2 is None else s2 + a2
    m1 = s1 * (1.0 / D_INNER)
    mu = shift + m1
    var = s2 * (1.0 / D_INNER) - m1 * m1
    rstd = lax.rsqrt(var + EPS)

    gpc = COL_CHUNK // MLP_GROUP_W
    for j in range(nch):
        cols = slice(j * COL_CHUNK, (j + 1) * COL_CHUNK)
        vn = (vbuf_ref[:, cols] - mu) * rstd * lng_ref[:, cols] + lnb_ref[:, cols]
        if emit_v:
            v_out_ref[:, cols] = vn
        vnb = vn.astype(BF16)
        mixed = []
        for gi in range(gpc):
            g = j * gpc + gi
            gcols = slice(gi * MLP_GROUP_W, (gi + 1) * MLP_GROUP_W)
            mixed.append(jnp.concatenate(
                [_dot(wmix_ref[g], vnb[r0:r0 + mix_rows, gcols]) + bmix_ref[g]
                 for r0 in range(0, rows_t, mix_rows)], axis=0))
        mixed = jnp.concatenate(mixed, axis=1)
        u = _gelu(_dot(h, win_ref[:, cols]))
        z = _dot(h, win_ref[:, 2 * D_INNER + j * COL_CHUNK:2 * D_INNER + (j + 1) * COL_CHUNK])
        y_ref[:, cols] = (u * mixed * _silu(z)).astype(BF16)

    _finish(x_ref, mod_ref, _dot(y_ref[...], wout_ref[...]), gfin_ref, o_ref, nseq, seg)


def _layer_b_kernel(nseq, seg, has_state, has_final, *refs):
    refs = list(refs)
    x_ref, mod_ref, g_ref = refs[:3]
    refs = refs[3:]
    state_ref = refs.pop(0) if has_state else None
    win_ref, wconv_ref, wout_ref = refs[:3]
    refs = refs[3:]
    gfin_ref = refs.pop(0) if has_final else None
    o_ref, state_out_ref, h_ref, ext_ref, y_ref = refs
    hist, hpad = SCONV_W - 1, SUBLANES
    nch = D_INNER // COL_CHUNK
    cc = COL_CHUNK

    _prenorm(x_ref, mod_ref, g_ref, h_ref, nseq, seg)
    _init_history(ext_ref, state_ref, hist, hpad, nseq)
    h = h_ref[...]

    for j in range(nch):
        cols = slice(j * cc, (j + 1) * cc)
        bg, cg, hv, z = [
            _dot(h, win_ref[:, br * D_INNER + j * cc:br * D_INNER + (j + 1) * cc]) for br in range(4)]
        p = cg * hv
        gate = bg * _silu(z)
        at = lambda s, r0, r1, cols=cols: (s, slice(r0, r1), cols)
        for s in range(nseq):
            ext_ref[s, hpad:hpad + seg, cols] = p[s * seg:(s + 1) * seg, :]
        if not has_state:
            _link_history(ext_ref, at, hist, hpad, nseq, seg)
        for s in range(nseq):
            rows = slice(s * seg, (s + 1) * seg)
            conv = p[rows, :] * wconv_ref[2:3, cols]
            for k in range(SCONV_W - 1):
                d = SCONV_W - 1 - k
                conv = conv + ext_ref[s, hpad - d:hpad - d + seg, cols] * wconv_ref[k:k + 1, cols]
            y_ref[rows, cols] = (conv * gate[rows, :]).astype(BF16)
        _roll_history(ext_ref, state_out_ref, at, lambda s, cols=cols: (s, slice(None), cols),
                      hist, hpad, nseq, seg, chained=not has_state)

    _finish(x_ref, mod_ref, _dot(y_ref[...], wout_ref[...]), gfin_ref, o_ref, nseq, seg)


def _layer_c_kernel(nseq, seg, has_state, has_final, *refs):
    refs = list(refs)
    x_ref, mod_ref, g_ref = refs[:3]
    refs = refs[3:]
    state_ref = refs.pop(0) if has_state else None
    win_ref, wconv_ref, bconv_ref, lng_ref, lnb_ref, wout_ref = refs[:6]
    refs = refs[6:]
    gfin_ref = refs.pop(0) if has_final else None
    o_ref, state_out_ref, h_ref, ext_ref, cbuf_ref, zs_ref, pbuf_ref = refs
    hist, hpad = CCONV_W - 1, _round_up(CCONV_W - 1, SUBLANES)
    nch = D_INNER // COL_CHUNK
    cc = COL_CHUNK
    pitch = _group_pitch(seg)
    pairs_per_chunk = cc // LANES // 2

    _init_slab_history(ext_ref, state_ref, hist, hpad, nseq, seg)
    _prenorm(x_ref, mod_ref, g_ref, h_ref, nseq, seg)

    h = h_ref[...]

    def proj_piece(branch, j, piece):
        lo = piece * MXU_COLS
        w_lo = branch * D_INNER + j * cc + lo
        pbuf_ref[branch, :, lo:lo + MXU_COLS] = _dot(h, win_ref[:, w_lo:w_lo + MXU_COLS])

    def conv_unit(j, s, unit, carry):
        p = j * pairs_per_chunk + unit
        c0 = 2 * p
        bias = bconv_ref[p]
        prev = None
        srcs = {}
        for v0 in range(0, pitch, VACC_UNITS):
            n_acc = min(VACC_UNITS, pitch - v0)
            if v0 % CONV_BLOCK == 0:
                srcs = {}
                for u in range(v0, min(v0 + CONV_BLOCK, pitch) + CCONV_W - 1):
                    start = hpad - (CCONV_W - 1) + u
                    srcs[u] = jnp.concatenate(
                        [ext_ref[s, c0, pl.ds(start, SUBLANES, stride=pitch), :],
                         ext_ref[s, c0 + 1, pl.ds(start, SUBLANES, stride=pitch), :]], axis=0).astype(BF16)
            accs = [None] * n_acc
            for k in range(CCONV_W):
                wk = wconv_ref[p, k]
                if k == 0 and prev is not None:
                    wk = pltpu.bitcast(pltpu.bitcast(wk, jnp.uint32) | _zero_of(prev), BF16)
                wk = wk.astype(F32)
                for i in range(n_acc):
                    term = srcs[v0 + i + k].astype(F32) * wk
                    accs[i] = term if accs[i] is None else accs[i] + term
            prev = accs[n_acc - 1][0:SUBLANES]
            for i in range(n_acc):
                res = accs[i] + bias
                cbuf_ref[s, c0, pl.ds(v0 + i, SUBLANES, stride=pitch), :] = res[0:SUBLANES]
                cbuf_ref[s, c0 + 1, pl.ds(v0 + i, SUBLANES, stride=pitch), :] = res[SUBLANES:2 * SUBLANES]
        return carry

    def read_conv(j):
        return jnp.concatenate(
            [jnp.concatenate([cbuf_ref[s, j * (cc // LANES) + ci, 0:seg, :] for ci in range(cc // LANES)], axis=1)
             for s in range(nseq)], axis=0)

    shift = s1 = s2 = None
    n_piece = cc // MXU_COLS
    for piece in range(n_piece):
        proj_piece(0, 0, piece)
        proj_piece(1, 0, piece)
    for j in range(nch):
        glu = pbuf_ref[0] * jax.nn.sigmoid(pbuf_ref[1])
        for s in range(nseq):
            for ci in range(cc // LANES):
                ext_ref[s, j * (cc // LANES) + ci, hpad:hpad + seg, :] = (
                    glu[s * seg:(s + 1) * seg, ci * LANES:(ci + 1) * LANES])
        if not has_state:
            for ci in range(cc // LANES):
                _link_history(ext_ref, functools.partial(_slab_rows, j * (cc // LANES) + ci), hist, hpad, nseq, seg)
        for piece in range(n_piece):
            proj_piece(2, j, piece)
        for s in range(nseq):
            lax.fori_loop(0, pairs_per_chunk, functools.partial(conv_unit, j, s), 0)
        for ci in range(cc // LANES):
            c = j * (cc // LANES) + ci
            _roll_history(ext_ref, state_out_ref, functools.partial(_slab_rows, c),
                          functools.partial(_slab_lanes, c), hist, hpad, nseq, seg, chained=not has_state)
        zs_ref[:, j * cc:(j + 1) * cc] = _silu(pbuf_ref[2]).astype(BF16)
        if j + 1 < nch:
            for piece in range(n_piece):
                proj_piece(0, j + 1, piece)
                proj_piece(1, j + 1, piece)
        cv = read_conv(j)
        if j == 0:
            shift = jnp.mean(cv, axis=-1, keepdims=True)
        dv = cv - shift
        a1 = jnp.sum(dv, axis=-1, keepdims=True)
        a2 = jnp.sum(dv * dv, axis=-1, keepdims=True)
        s1 = a1 if s1 is None else s1 + a1
        s2 = a2 if s2 is None else s2 + a2
    m1 = s1 * (1.0 / D_INNER)
    mu = shift + m1
    var = s2 * (1.0 / D_INNER) - m1 * m1
    rstd = lax.rsqrt(var + EPS)

    out = None
    for j in range(nch):
        cols = slice(j * cc, (j + 1) * cc)
        yn = (read_conv(j) - mu) * rstd * lng_ref[:, cols] + lnb_ref[:, cols]
        part = _dot((_silu(yn) * zs_ref[:, cols].astype(F32)).astype(BF16), wout_ref[cols, :])
        out = part if out is None else out + part

    _finish(x_ref, mod_ref, out, gfin_ref, o_ref, nseq, seg)


def _layer_d_kernel(nseq, seg, has_state, has_final, *refs):
    refs = list(refs)
    x_ref, mod_ref, g_ref = refs[:3]
    refs = refs[3:]
    state_ref = refs.pop(0) if has_state else None
    win_ref, wpool_ref, scale_ref, wout_ref = refs[:4]
    refs = refs[4:]
    gfin_ref = refs.pop(0) if has_final else None
    o_ref, state_out_ref, h_ref, ext_ref, dbuf_ref, y_ref = refs
    hist, hpad = POOL_HIST, _round_up(POOL_HIST, SUBLANES)
    cc = POOL_GROUP_W
    pitch = _group_pitch(seg)
    spg = cc // LANES

    _init_slab_history(ext_ref, state_ref, hist, hpad, nseq, seg)
    _prenorm(x_ref, mod_ref, g_ref, h_ref, nseq, seg)
    h = h_ref[...]

    lane_row = lax.broadcasted_iota(jnp.int32, (SUBLANES, LANES), 0) * pitch

    for gidx, win in enumerate(POOL_WINDOWS):
        cols = slice(gidx * cc, (gidx + 1) * cc)
        pv = _dot(h, win_ref[:, cols])
        z = _dot(h, win_ref[:, D_INNER + gidx * cc:D_INNER + (gidx + 1) * cc])
        for s in range(nseq):
            for ci in range(spg):
                ext_ref[s, gidx * spg + ci, hpad:hpad + seg, :] = (
                    pv[s * seg:(s + 1) * seg, ci * LANES:(ci + 1) * LANES])
        if not has_state:
            for ci in range(spg):
                _link_history(ext_ref, functools.partial(_slab_rows, gidx * spg + ci), hist, hpad, nseq, seg)
        for s in range(nseq):
            pos0 = PAST_LEN if has_state else pl.program_id(1) * (nseq * seg) + s * seg

            def rows_at(ci, r, s=s):
                return ext_ref[s, gidx * spg + ci, pl.ds(hpad + r, SUBLANES, stride=pitch), :]

            tots = [None] * spg
            for v in range(pitch):
                inv_cnt = 1.0 / jnp.minimum(win, lane_row + (pos0 + v + 1)).astype(F32)
                for ci in range(spg):
                    cur = rows_at(ci, v)
                    if v == 0:
                        tots[ci] = cur
                        for d in range(1, win):
                            tots[ci] = tots[ci] + rows_at(ci, -d)
                    else:
                        tots[ci] = tots[ci] + cur - rows_at(ci, v - win)
                    dbuf_ref[s, ci, pl.ds(v, SUBLANES, stride=pitch), :] = tots[ci] * inv_cnt - cur
        for ci in range(spg):
            c = gidx * spg + ci
            _roll_history(ext_ref, state_out_ref, functools.partial(_slab_rows, c),
                          functools.partial(_slab_lanes, c), hist, hpad, nseq, seg, chained=not has_state)
        dlt = jnp.concatenate(
            [jnp.concatenate([dbuf_ref[s, ci, 0:seg, :] for ci in range(spg)], axis=1) for s in range(nseq)],
            axis=0)
        yp = _dot(dlt.astype(BF16), wpool_ref[gidx]) * scale_ref[:, cols]
        y_ref[:, cols] = (yp * _silu(z)).astype(BF16)

    _finish(x_ref, mod_ref, _dot(y_ref[...], wout_ref[...]), gfin_ref, o_ref, nseq, seg)


def _const_spec(arr):
    nd = arr.ndim
    return pl.BlockSpec(arr.shape, lambda b, t, _nd=nd: (0,) * _nd, pipeline_mode=pl.Buffered(1))


def _run_layer(body, name, x, mod, g, state, weights, g_final, hist, hpad, nseq, seg, n_batch_steps,
               n_tiles, chained, extra_scratch, emit_v=False, with_state_out=True, ext_shape=None, stage_y=True):
    rows_t = nseq * seg
    m_rows = x.shape[0]
    seq_blk = 1 if chained else nseq
    n_seq_total = n_batch_steps * seq_blk
    in_arrays = [x, mod, g]
    in_specs = [
        pl.BlockSpec((rows_t, D_MODEL), lambda b, t: (b * n_tiles + t, 0)),
        pl.BlockSpec((seq_blk, 3, D_MODEL), lambda b, t: (b, 0, 0)),
        _const_spec(g),
    ]
    if state is not None:
        in_arrays.append(state)
        in_specs.append(_const_spec(state))
    for w in weights:
        in_arrays.append(w)
        in_specs.append(_const_spec(w))
    if g_final is not None:
        in_arrays.append(g_final)
        in_specs.append(_const_spec(g_final))

    out_shapes = [jax.ShapeDtypeStruct((m_rows, D_MODEL), F32)]
    out_specs = [pl.BlockSpec((rows_t, D_MODEL), lambda b, t: (b * n_tiles + t, 0))]
    if with_state_out:
        out_shapes.append(jax.ShapeDtypeStruct((n_seq_total, hist, D_INNER), F32))
        out_specs.append(pl.BlockSpec((seq_blk, hist, D_INNER), lambda b, t: (b, 0, 0)))
    if emit_v:
        out_shapes.append(jax.ShapeDtypeStruct((m_rows, D_INNER), F32))
        out_specs.append(pl.BlockSpec((rows_t, D_INNER), lambda b, t: (b * n_tiles + t, 0)))

    scratch = [pltpu.VMEM((rows_t, D_MODEL), BF16)]
    if with_state_out:
        scratch.append(pltpu.VMEM(ext_shape or (nseq, hpad + seg, D_INNER), F32))
    scratch += extra_scratch
    if stage_y:
        scratch.append(pltpu.VMEM((rows_t, D_INNER), BF16))

    return pl.pallas_call(
        body,
        grid=(n_batch_steps, n_tiles),
        in_specs=in_specs,
        out_specs=out_specs,
        out_shape=out_shapes,
        scratch_shapes=scratch,
        compiler_params=pltpu.CompilerParams(
            dimension_semantics=("arbitrary", "arbitrary"), vmem_limit_bytes=VMEM_LIMIT_BYTES),
        name=name,
    )(*in_arrays)


def _pack_conv_weights(w):
    taps = w.shape[0]
    pairs = D_INNER // (2 * LANES)
    w = w.reshape(taps, pairs, 2, 1, LANES).transpose(1, 0, 2, 3, 4)
    w = jnp.broadcast_to(w, (pairs, taps, 2, SUBLANES, LANES))
    return w.reshape(pairs, taps, 2 * SUBLANES, LANES)


def _mix_matrices(w_s_a, b_s_a, seg):
    blk = MLP_BLOCK if seg >= MLP_BLOCK else seg
    cidx = jnp.arange(blk) // CHUNK
    mask = cidx[:, None] >= cidx[None, :]
    w = jnp.where(mask[None], w_s_a[:, :blk, :blk], jnp.zeros((), w_s_a.dtype))
    reps = MIX_ROWS // blk
    eye = jnp.eye(reps, dtype=w.dtype)
    wmix = jnp.einsum('rs,gij->grisj', eye, w).reshape(MLP_GROUPS, reps * blk, reps * blk)
    bmix = jnp.tile(b_s_a[:, :blk], (1, reps))[:, :, None]
    return wmix.astype(BF16), bmix.astype(F32)


def _trunk(x, mod, state_b, state_c, state_d, nseq, seg, n_batch_steps, n_tiles, emit_v, prm):
    has_state = state_b is not None
    rows_t = nseq * seg
    row = lambda v: v.reshape(1, -1)
    wmix, bmix = _mix_matrices(prm["w_s_a"], prm["b_s_a"], seg)
    common = dict(nseq=nseq, seg=seg, n_batch_steps=n_batch_steps, n_tiles=n_tiles, chained=not has_state)

    res = _run_layer(
        functools.partial(_layer_a_kernel, nseq, seg, emit_v, False), "layer_a",
        x, mod[0], row(prm["g_norm"][0]), None,
        [prm["wa_in"], row(prm["ln_a_g"]), row(prm["ln_a_b"]), wmix, bmix, prm["wa_out"]],
        None, 0, 0, extra_scratch=[pltpu.VMEM((rows_t, D_INNER), F32)], emit_v=emit_v,
        with_state_out=False, **common)
    x = res[0]
    v_rows = res[1] if emit_v else None

    hist, hpad = SCONV_W - 1, SUBLANES
    x, conv_b = _run_layer(
        functools.partial(_layer_b_kernel, nseq, seg, has_state, False), "layer_b",
        x, mod[1], row(prm["g_norm"][1]), state_b,
        [prm["wb_in"], prm["w_conv_b"], prm["wb_out"]],
        None, hist, hpad, extra_scratch=[], **common)

    hist, hpad = CCONV_W - 1, _round_up(CCONV_W - 1, SUBLANES)
    pitch = _group_pitch(seg)
    x, conv_c = _run_layer(
        functools.partial(_layer_c_kernel, nseq, seg, has_state, False), "layer_c",
        x, mod[2], row(prm["g_norm"][2]), state_c,
        [prm["wc_in"], prm["wc_conv"], prm["bc_conv"], row(prm["ln_c_g"]),
         row(prm["ln_c_b"]), prm["wc_out"]],
        None, hist, hpad,
        extra_scratch=[pltpu.VMEM((nseq, D_INNER // LANES, SUBLANES * pitch, LANES), F32),
                       pltpu.VMEM((rows_t, D_INNER), BF16),
                       pltpu.VMEM((3, rows_t, COL_CHUNK), F32)],
        ext_shape=(nseq, D_INNER // LANES, hpad + SUBLANES * pitch, LANES), stage_y=False, **common)

    hist, hpad = POOL_HIST, _round_up(POOL_HIST, SUBLANES)
    y, pool_d = _run_layer(
        functools.partial(_layer_d_kernel, nseq, seg, has_state, True), "layer_d",
        x, mod[3], row(prm["g_norm"][3]), state_d,
        [prm["wd_in"], prm["wd_pool"], row(prm["scale_pool_d"]), prm["wd_out"]],
        row(prm["g_final"]), hist, hpad,
        extra_scratch=[pltpu.VMEM((nseq, POOL_GROUP_W // LANES, SUBLANES * pitch, LANES), F32)],
        ext_shape=(nseq, D_INNER // LANES, hpad + SUBLANES * pitch, LANES), **common)
    return y, v_rows, conv_b, conv_c, pool_d


def kernel(x_prompt, x_sample, state_conv_b, state_conv_c, state_pool_d, c_prompt, c_sample, w_ada, b_ada, g_norm, w_in_a, ln_a_g, ln_a_b, w_s_a, b_s_a, w_out_a, w_in_b, w_conv_b, w_out_b, w_in_c, w_conv_c, b_conv_c, ln_c_g, ln_c_b, w_out_c, w_in_d, w_pool_d, scale_pool_d, w_out_d, g_final):
    n_p, s_p, _ = x_prompt.shape
    n_s, s_s, _ = x_sample.shape

    c_all = jnp.concatenate([c_prompt, c_sample], axis=0)
    pad = _round_up(n_p + n_s, SUBLANES) - (n_p + n_s)
    c_all = jnp.pad(c_all, ((0, pad), (0, 0)))
    mod = _modulation(c_all, w_ada, b_ada)
    mod_p = mod[:, :n_p].reshape(DEPTH, n_p, 3, D_MODEL)
    mod_s = mod[:, n_p:n_p + n_s].reshape(DEPTH, n_s, 3, D_MODEL)

    prm = dict(
        g_norm=g_norm, ln_a_g=ln_a_g, ln_a_b=ln_a_b, w_s_a=w_s_a, b_s_a=b_s_a,
        w_conv_b=w_conv_b, ln_c_g=ln_c_g, ln_c_b=ln_c_b,
        scale_pool_d=scale_pool_d, g_final=g_final,
        wa_in=w_in_a.astype(BF16),
        wa_out=w_out_a.astype(BF16),
        wb_in=w_in_b.astype(BF16),
        wb_out=w_out_b.astype(BF16),
        wc_in=w_in_c.astype(BF16),
        wc_conv=_pack_conv_weights(w_conv_c).astype(BF16),
        bc_conv=_pack_conv_weights(b_conv_c.reshape(1, -1))[:, 0],
        wc_out=w_out_c.astype(BF16),
        wd_in=w_in_d.astype(BF16),
        wd_pool=w_pool_d.astype(BF16),
        wd_out=w_out_d.astype(BF16),
    )

    y_p, _, cb_p, cc_p, pd_p = _trunk(
        x_prompt.reshape(n_p * s_p, D_MODEL), mod_p, None, None, None,
        nseq=ROW_TILE // SEG_ROWS, seg=SEG_ROWS, n_batch_steps=n_p, n_tiles=s_p // ROW_TILE, emit_v=False, prm=prm)
    y_s, v_s, cb_s, cc_s, pd_s = _trunk(
        x_sample.reshape(n_s * s_s, D_MODEL), mod_s, state_conv_b, state_conv_c, state_pool_d,
        nseq=n_s, seg=s_s, n_batch_steps=1, n_tiles=1, emit_v=True, prm=prm)

    return (y_p.reshape(n_p, s_p, D_MODEL), y_s.reshape(n_s, s_s, D_MODEL),
            v_s.reshape(n_s, s_s, D_INNER), cb_p, cb_s, cc_p, cc_s, pd_p, pd_s)
```
